```python
import math
import jax, jax.numpy as jnp
from jax import lax
import numpy as np

D_MODEL = 2048
BATCH = 4
SEQ = 4096
DEPTH = 4

N_META = 16
EPS = 1e-6
ML_HEADS = 8
ML_DV = D_MODEL // ML_HEADS
ML_DQK = ML_DV // 2
ML_CHUNK = 64
GATE_CAP = 15.0
ML_QK_W = ML_HEADS * ML_DQK
ML_V_W = ML_HEADS * ML_DV
ML_IN = 2 * ML_QK_W + 2 * ML_V_W + 2 * ML_HEADS
ML_SPLITS = (ML_QK_W, 2 * ML_QK_W, 2 * ML_QK_W + ML_V_W,
             2 * ML_QK_W + 2 * ML_V_W, 2 * ML_QK_W + 2 * ML_V_W + ML_HEADS)
ML_PAD = (-N_META) % ML_CHUNK
NEG_BIG = -1e30
FOX_HEADS = 32
FOX_DH = D_MODEL // FOX_HEADS
FOX_IN = 4 * D_MODEL + FOX_HEADS
FOX_SPLITS = (D_MODEL, 2 * D_MODEL, 3 * D_MODEL, 4 * D_MODEL)
Q_BLOCK = 128
D_FF = 11 * D_MODEL // 4
N_EXPERTS = 8
TOP_K = 2
D_FF_EXPERT = 11 * D_MODEL // 4

kernel_name = "hybrid_mlstm_fox_moe_trunk"


def rmsnorm(x, g):
    xf = x.astype(jnp.float32)
    y = xf * lax.rsqrt(jnp.mean(xf * xf, axis=-1, keepdims=True) + EPS)
    return (y * g.astype(jnp.float32)).astype(x.dtype)


def softcap(z):
    return GATE_CAP * jnp.tanh(z / GATE_CAP)


def swiglu(h, w_gate, w_up, w_down):
    return (jax.nn.silu(h @ w_gate) * (h @ w_up)) @ w_down


def mlstm_mixer(h, w_in, b_i, b_f, h_gain, w_out):
    B, T, _ = h.shape
    f32 = jnp.float32
    H, L = ML_HEADS, ML_CHUNK
    q, k, v, o, ig, fg = jnp.split(h @ w_in, ML_SPLITS, axis=-1)

    def heads(t, d):
        t = t.reshape(B, T, H, d).transpose(0, 2, 1, 3).astype(f32)
        return jnp.pad(t, ((0, 0), (0, 0), (ML_PAD, 0), (0, 0)))

    q = heads(q, ML_DQK)
    k = heads(k, ML_DQK) * (ML_DQK ** -0.5)
    v = heads(v, ML_DV)
    li = softcap(ig.astype(f32) + b_i.astype(f32)).transpose(0, 2, 1)
    lf = jax.nn.log_sigmoid(softcap(fg.astype(f32) + b_f.astype(f32))).transpose(0, 2, 1)
    li = jnp.pad(li, ((0, 0), (0, 0), (ML_PAD, 0)), constant_values=NEG_BIG)
    lf = jnp.pad(lf, ((0, 0), (0, 0), (ML_PAD, 0)))
    Tp = T + ML_PAD
    NC = Tp // L
    q = q.reshape(B, H, NC, L, ML_DQK)
    k = k.reshape(B, H, NC, L, ML_DQK)
    v = v.reshape(B, H, NC, L, ML_DV)
    li = li.reshape(B, H, NC, L)
    lf = lf.reshape(B, H, NC, L)

    b = jnp.cumsum(lf, axis=-1)
    g = b[..., -1]
    a = g[..., None] - b + li
    m_loc = jnp.max(a, axis=-1)
    w_loc = jnp.exp(a - m_loc[..., None])
    C_loc = jnp.einsum('bhcs,bhcsk,bhcsv->bhckv', w_loc, k, v)
    n_loc = jnp.einsum('bhcs,bhcsk->bhck', w_loc, k)

    def step(carry, inp):
        C, n, m = carry
        Cl, nl, ml, gl = inp
        m_new = jnp.maximum(gl + m, ml)
        sp = jnp.exp(gl + m - m_new)
        sl = jnp.exp(ml - m_new)
        C_new = sp[..., None, None] * C + sl[..., None, None] * Cl
        n_new = sp[..., None] * n + sl[..., None] * nl
        return (C_new, n_new, m_new), (C, n, m)

    init = (jnp.zeros((B, H, ML_DQK, ML_DV), f32), jnp.zeros((B, H, ML_DQK), f32), jnp.zeros((B, H), f32))
    xs = (jnp.moveaxis(C_loc, 2, 0), jnp.moveaxis(n_loc, 2, 0), jnp.moveaxis(m_loc, 2, 0), jnp.moveaxis(g, 2, 0))
    _, (C_prev, n_prev, m_prev) = lax.scan(step, init, xs)
    C_prev = jnp.moveaxis(C_prev, 0, 2)
    n_prev = jnp.moveaxis(n_prev, 0, 2)
    m_prev = jnp.moveaxis(m_prev, 0, 2)

    causal = jnp.tril(jnp.ones((L, L), dtype=bool))
    d_log = jnp.where(causal, b[..., :, None] - b[..., None, :] + li[..., None, :], -jnp.inf)
    inter = b + m_prev[..., None]
    m_t = jnp.maximum(inter, jnp.max(d_log, axis=-1))
    s = jnp.einsum('bhctk,bhcsk->bhcts', q, k) * jnp.exp(d_log - m_t[..., None])
    w_inter = jnp.exp(inter - m_t)
    num = jnp.einsum('bhcts,bhcsv->bhctv', s, v) + w_inter[..., None] * jnp.einsum('bhctk,bhckv->bhctv', q, C_prev)
    den = jnp.sum(s, axis=-1) + w_inter * jnp.einsum('bhctk,bhck->bhct', q, n_prev)
    hh = num / jnp.maximum(jnp.abs(den), jnp.exp(-m_t))[..., None]
    hh = hh.reshape(B, H, Tp, ML_DV)[:, :, ML_PAD:]
    hh = rmsnorm(hh, h_gain.reshape(H, 1, ML_DV))
    hh = hh.transpose(0, 2, 1, 3).reshape(B, T, ML_V_W).astype(h.dtype)
    return (jax.nn.sigmoid(o) * hh) @ w_out


def fox_mixer(h, w_in, b_f, q_gain, k_gain, w_out):
    B, T, _ = h.shape
    f32 = jnp.float32
    H, dh = FOX_HEADS, FOX_DH
    q, k, v, og, fg = jnp.split(h @ w_in, FOX_SPLITS, axis=-1)

    def heads(t):
        return t.reshape(B, T, H, dh).transpose(0, 2, 1, 3)

    q = rmsnorm(heads(q), q_gain) * (dh ** -0.5)
    k = rmsnorm(heads(k), k_gain)
    v = heads(v)
    c = jnp.cumsum(jax.nn.log_sigmoid(fg.astype(f32) + b_f.astype(f32)), axis=1).transpose(0, 2, 1)
    pos = jnp.arange(T)

    def attend(q_blk, c_blk, t_idx):
        logits = jnp.einsum('bhtd,bhsd->bhts', q_blk, k).astype(f32) + (c_blk[..., :, None] - c[..., None, :])
        logits = jnp.where(t_idx[:, None] >= pos[None, :], logits, -jnp.inf)
        p = jax.nn.softmax(logits, axis=-1)
        return jnp.einsum('bhts,bhsd->bhtd', p.astype(v.dtype), v)

    y_meta = attend(q[:, :, :N_META], c[:, :, :N_META], pos[:N_META])
    nb = (T - N_META) // Q_BLOCK
    q_blocks = q[:, :, N_META:].reshape(B, H, nb, Q_BLOCK, dh).transpose(2, 0, 1, 3, 4)
    c_blocks = c[:, :, N_META:].reshape(B, H, nb, Q_BLOCK).transpose(2, 0, 1, 3)
    t_blocks = (N_META + jnp.arange(nb * Q_BLOCK)).reshape(nb, Q_BLOCK)
    y_real = lax.map(lambda args: attend(*args), (q_blocks, c_blocks, t_blocks))
    y_real = y_real.transpose(1, 2, 0, 3, 4).reshape(B, H, T - N_META, dh)
    y = jnp.concatenate([y_meta, y_real], axis=2).transpose(0, 2, 1, 3).reshape(B, T, D_MODEL)
    return (jax.nn.sigmoid(og) * y) @ w_out


def moe_ffn(h, router, w_gate, w_up, w_down):
    B, T, D = h.shape
    hf = h.reshape(B * T, D)
    logits = (hf @ router).astype(jnp.float32)
    top_v, top_i = lax.top_k(logits, TOP_K)
    gates = jax.nn.softmax(top_v, axis=-1)
    combine = jnp.einsum('nk,nke->ne', gates, jax.nn.one_hot(top_i, N_EXPERTS, dtype=jnp.float32)).astype(h.dtype)
    y = jnp.zeros_like(hf)
    for e in range(N_EXPERTS):
        y = y + combine[:, e:e + 1] * swiglu(hf, w_gate[e], w_up[e], w_down[e])
    return y.reshape(B, T, D)


def setup_inputs(seed: int = 0) -> dict:
    key = jax.random.key(seed)
    ks = jax.random.split(key, 32)
    f32 = jnp.float32
    D = D_MODEL
    n_even = (DEPTH + 1) // 2
    n_odd = DEPTH // 2
    out_scale = (2.0 * DEPTH) ** -0.5

    def nrm(k, shape, fan_in, scale=1.0):
        return jax.random.normal(k, shape, f32) * (fan_in ** -0.5) * scale

    def gain(k, shape):
        return 1.0 + 0.02 * jax.random.normal(k, shape, f32)

    x = jax.random.normal(ks[0], (BATCH, SEQ, D), f32)
    meta_tokens = jax.random.normal(ks[1], (N_META, D), f32)
    ml_norm = gain(ks[2], (n_even, D))
    ml_w_in = nrm(ks[3], (n_even, D, ML_IN), D)
    ml_b_i = 0.1 * jax.random.normal(ks[4], (n_even, ML_HEADS), f32)
    ml_b_f = jnp.linspace(3.0, 6.0, ML_HEADS, dtype=f32)[None, :] + 0.1 * jax.random.normal(ks[5], (n_even, ML_HEADS), f32)
    ml_h_gain = gain(ks[6], (n_even, ML_V_W))
    ml_w_out = nrm(ks[7], (n_even, ML_V_W, D), ML_V_W, out_scale)
    ffn_norm = gain(ks[8], (n_even, D))
    ffn_w_gate = nrm(ks[9], (n_even, D, D_FF), D)
    ffn_w_up = nrm(ks[10], (n_even, D, D_FF), D)
    ffn_w_down = nrm(ks[11], (n_even, D_FF, D), D_FF, out_scale)
    fox_norm = gain(ks[12], (n_odd, D))
    fox_w_in = nrm(ks[13], (n_odd, D, FOX_IN), D)
    fox_b_f = jnp.linspace(1.0, 5.0, FOX_HEADS, dtype=f32)[None, :] + 0.1 * jax.random.normal(ks[14], (n_odd, FOX_HEADS), f32)
    fox_q_gain = gain(ks[15], (n_odd, FOX_DH))
    fox_k_gain = gain(ks[16], (n_odd, FOX_DH))
    fox_w_out = nrm(ks[17], (n_odd, D, D), D, out_scale)
    moe_norm = gain(ks[18], (n_odd, D))
    moe_router = nrm(ks[19], (n_odd, D, N_EXPERTS), D)
    moe_w_gate = nrm(ks[20], (n_odd, N_EXPERTS, D, D_FF_EXPERT), D)
    moe_w_up = nrm(ks[21], (n_odd, N_EXPERTS, D, D_FF_EXPERT), D)
    moe_w_down = nrm(ks[22], (n_odd, N_EXPERTS, D_FF_EXPERT, D), D_FF_EXPERT, out_scale)
    final_norm = gain(ks[23], (D,))
    return {"x": x, "meta_tokens": meta_tokens,
            "ml_norm": ml_norm, "ml_w_in": ml_w_in, "ml_b_i": ml_b_i, "ml_b_f": ml_b_f,
            "ml_h_gain": ml_h_gain, "ml_w_out": ml_w_out,
            "ffn_norm": ffn_norm, "ffn_w_gate": ffn_w_gate, "ffn_w_up": ffn_w_up, "ffn_w_down": ffn_w_down,
            "fox_norm": fox_norm, "fox_w_in": fox_w_in, "fox_b_f": fox_b_f,
            "fox_q_gain": fox_q_gain, "fox_k_gain": fox_k_gain, "fox_w_out": fox_w_out,
            "moe_norm": moe_norm, "moe_router": moe_router, "moe_w_gate": moe_w_gate,
            "moe_w_up": moe_w_up, "moe_w_down": moe_w_down,
            "final_norm": final_norm}


def reference(x, meta_tokens,
              ml_norm, ml_w_in, ml_b_i, ml_b_f, ml_h_gain, ml_w_out,
              ffn_norm, ffn_w_gate, ffn_w_up, ffn_w_down,
              fox_norm, fox_w_in, fox_b_f, fox_q_gain, fox_k_gain, fox_w_out,
              moe_norm, moe_router, moe_w_gate, moe_w_up, moe_w_down,
              final_norm):
    B = x.shape[0]
    meta = jnp.broadcast_to(meta_tokens[None].astype(x.dtype), (B, N_META, D_MODEL))
    h = jnp.concatenate([meta, x], axis=1)
    for i in range(DEPTH):
        j = i // 2
        if i % 2 == 0:
            h = h + mlstm_mixer(rmsnorm(h, ml_norm[j]), ml_w_in[j], ml_b_i[j], ml_b_f[j], ml_h_gain[j], ml_w_out[j])
            h = h + swiglu(rmsnorm(h, ffn_norm[j]), ffn_w_gate[j], ffn_w_up[j], ffn_w_down[j])
        else:
            h = h + fox_mixer(rmsnorm(h, fox_norm[j]), fox_w_in[j], fox_b_f[j], fox_q_gain[j], fox_k_gain[j], fox_w_out[j])
            h = h + moe_ffn(rmsnorm(h, moe_norm[j]), moe_router[j], moe_w_gate[j], moe_w_up[j], moe_w_down[j])
    return rmsnorm(h[:, N_META:], final_norm)
```

```python
import functools

import jax
import jax.numpy as jnp
from jax import lax
from jax.experimental import pallas as pl
from jax.experimental.pallas import tpu as pltpu

F32 = jnp.float32
BF16 = jnp.bfloat16

N_META = 16
EPS = 1e-6
LANE = 128
FRONT = 128
N_DUMMY = FRONT - N_META
ML_HEADS = 8
ML_DQK = 128
ML_DV = 256
ML_CHUNK = 128
GATE_CAP = 15.0
NEG_BIG = -1e30
FOX_HEADS = 32
FOX_DH = 64
FOX_PAIRS = FOX_HEADS // 2
N_EXPERTS = 8
TOP_K = 2

VMEM_LIMIT = 56 * 1024 * 1024


def _params(sem, vmem=VMEM_LIMIT):
    return pltpu.CompilerParams(dimension_semantics=sem, vmem_limit_bytes=vmem)


def _tile(n, pref):
    best = None
    t = LANE
    while t <= min(n, pref):
        if n % t == 0:
            best = t
        t += LANE
    assert best is not None, (n, pref)
    return best


def _split3(x):
    x1 = x.astype(BF16)
    r1 = x - x1.astype(F32)
    x2 = r1.astype(BF16)
    r2 = r1 - x2.astype(F32)
    return x1, x2, r2.astype(BF16)


def _log_sigmoid(x):
    return -(jnp.maximum(-x, 0.0) + jnp.log1p(jnp.exp(-jnp.abs(x))))


def _sigmoid(x):
    return 1.0 / (1.0 + jnp.exp(-x))


def _rmsnorm_kernel(x_ref, g_ref, o_ref):
    x = x_ref[...]
    ms = jnp.mean(x * x, axis=-1, keepdims=True)
    o_ref[...] = (x * lax.rsqrt(ms + EPS) * g_ref[...]).astype(o_ref.dtype)


def rmsnorm(x, g, out_dtype):
    n, d = x.shape
    tm = _tile(n, 512)
    return pl.pallas_call(
        _rmsnorm_kernel,
        grid=(n // tm,),
        in_specs=[pl.BlockSpec((tm, d), lambda i: (i, 0)),
                  pl.BlockSpec((1, d), lambda i: (0, 0))],
        out_specs=pl.BlockSpec((tm, d), lambda i: (i, 0)),
        out_shape=jax.ShapeDtypeStruct((n, d), out_dtype),
        compiler_params=_params(("parallel",)),
        name="rmsnorm",
    )(x, g.reshape(1, d))


def _final_norm_kernel(x_ref, g_ref, o_ref):
    x = x_ref[0]
    ms = jnp.mean(x * x, axis=-1, keepdims=True)
    o_ref[0] = x * lax.rsqrt(ms + EPS) * g_ref[...]


def final_rmsnorm(h3, g):
    b, tp, d = h3.shape
    nblk = (tp - FRONT) // LANE
    return pl.pallas_call(
        _final_norm_kernel,
        grid=(b, nblk),
        in_specs=[pl.BlockSpec((1, LANE, d), lambda i, j: (i, j + FRONT // LANE, 0)),
                  pl.BlockSpec((1, d), lambda i, j: (0, 0))],
        out_specs=pl.BlockSpec((1, LANE, d), lambda i, j: (i, j, 0)),
        out_shape=jax.ShapeDtypeStruct((b, tp - FRONT, d), F32),
        compiler_params=_params(("parallel", "parallel")),
        name="final_norm",
    )(h3, g.reshape(1, d))


def _group_sum_matrix():
    r = lax.broadcasted_iota(jnp.int32, (LANE, LANE), 0) // FOX_DH
    c = lax.broadcasted_iota(jnp.int32, (LANE, LANE), 1) // FOX_DH
    return jnp.where(r == c, 1.0, 0.0).astype(BF16)


def _matmul_kernel(*refs, has_resid, n_norm_tiles):
    x_ref, w_ref = refs[0], refs[1]
    k = 2
    gain_ref = resid_ref = None
    if n_norm_tiles:
        gain_ref = refs[k]; k += 1
    if has_resid:
        resid_ref = refs[k]; k += 1
    o_ref = refs[k]
    acc = jnp.dot(x_ref[...], w_ref[...], preferred_element_type=F32)
    if has_resid:
        acc = resid_ref[...] + acc

    if not n_norm_tiles:
        o_ref[...] = acc.astype(o_ref.dtype)
        return

    j = pl.program_id(0)

    @pl.when(j >= n_norm_tiles)
    def _():
        o_ref[...] = acc.astype(o_ref.dtype)

    @pl.when(j < n_norm_tiles)
    def _():
        gmat = _group_sum_matrix()
        tn = acc.shape[1]
        for c in range(tn // LANE):
            y = acc[:, c * LANE:(c + 1) * LANE]
            y2 = y * y
            hi = y2.astype(BF16)
            lo = (y2 - hi.astype(F32)).astype(BF16)
            ss = (jnp.dot(hi, gmat, preferred_element_type=F32)
                  + jnp.dot(lo, gmat, preferred_element_type=F32))
            yn = y * lax.rsqrt(ss * (1.0 / FOX_DH) + EPS) * gain_ref[:, c * LANE:(c + 1) * LANE]
            o_ref[:, c * LANE:(c + 1) * LANE] = yn.astype(o_ref.dtype)


def matmul(x, w, n_cols, out_dtype, resid=None, head_gain=None, n_norm_cols=0, tm_pref=512, tn_pref=1024):
    m, kdim = x.shape
    tm = _tile(m, tm_pref)
    tn = _tile(n_cols, tn_pref)
    n_norm_tiles = n_norm_cols // tn
    assert n_norm_tiles * tn == n_norm_cols
    in_specs = [pl.BlockSpec((tm, kdim), lambda j, i: (i, 0)),
                pl.BlockSpec((kdim, tn), lambda j, i: (0, j))]
    args = [x, w]
    if n_norm_tiles:
        in_specs.append(pl.BlockSpec((1, tn), lambda j, i: (0, j)))
        args.append(head_gain)
    if resid is not None:
        in_specs.append(pl.BlockSpec((tm, tn), lambda j, i: (i, j)))
        args.append(resid)
    return pl.pallas_call(
        functools.partial(_matmul_kernel, has_resid=resid is not None, n_norm_tiles=n_norm_tiles),
        grid=(n_cols // tn, m // tm),
        in_specs=in_specs,
        out_specs=pl.BlockSpec((tm, tn), lambda j, i: (i, j)),
        out_shape=jax.ShapeDtypeStruct((m, n_cols), out_dtype),
        compiler_params=_params(("parallel", "parallel")),
        name="matmul",
    )(*args)


def _gate_proj_kernel(w_ref, x_ref, o_ref):
    o_ref[...] = lax.dot_general(w_ref[...], x_ref[...], (((1,), (1,)), ((), ())),
                                 preferred_element_type=F32)


def gate_proj(x, w_t):
    m, kdim = x.shape
    g = w_t.shape[0]
    tm = _tile(m, 512)
    return pl.pallas_call(
        _gate_proj_kernel,
        grid=(m // tm,),
        in_specs=[pl.BlockSpec((g, kdim), lambda i: (0, 0)),
                  pl.BlockSpec((tm, kdim), lambda i: (i, 0))],
        out_specs=pl.BlockSpec((g, tm), lambda i: (0, i)),
        out_shape=jax.ShapeDtypeStruct((g, m), F32),
        compiler_params=_params(("parallel",)),
        name="gate_proj",
    )(w_t, x)


def _mlstm_kernel(bi_ref, bf_ref, q_ref, k_ref, v_ref, og_ref, ig_ref, fg_ref, gain_ref,
                  out_ref, c_ref, m_ref):
    L = ML_CHUNK
    hd = pl.program_id(1)
    ci = pl.program_id(2)

    @pl.when(ci == 0)
    def _():
        c_ref[...] = jnp.zeros_like(c_ref)
        m_ref[...] = jnp.zeros_like(m_ref)

    pos = ci * L + lax.broadcasted_iota(jnp.int32, (1, L), 1)
    dummy = pos < N_DUMMY
    ig = ig_ref[0, 0] + bi_ref[hd]
    fg = fg_ref[0, 0] + bf_ref[hd]
    li = jnp.where(dummy, NEG_BIG, GATE_CAP * jnp.tanh(ig / GATE_CAP))
    lf = jnp.where(dummy, 0.0, _log_sigmoid(GATE_CAP * jnp.tanh(fg / GATE_CAP)))

    t_i = lax.broadcasted_iota(jnp.int32, (L, L), 0)
    s_i = lax.broadcasted_iota(jnp.int32, (L, L), 1)
    tril = s_i <= t_i
    b_col = jnp.sum(jnp.where(tril, jnp.broadcast_to(lf, (L, L)), 0.0), axis=1, keepdims=True)
    b_t = jnp.broadcast_to(b_col, (L, L))
    b_s = b_t.T
    d_log = jnp.where(tril, b_t - b_s + li, -jnp.inf)
    m_prev = m_ref[0:1, 0:1]
    inter = b_col + m_prev
    m_t = jnp.maximum(inter, jnp.max(d_log, axis=1, keepdims=True))

    q = q_ref[0]
    k = k_ref[0]
    scale = ML_DQK ** -0.5
    s = lax.dot_general(q, k, (((1,), (1,)), ((), ())), preferred_element_type=F32) * scale
    p = (s * jnp.exp(d_log - m_t)).astype(BF16)
    w_inter = jnp.exp(inter - m_t)

    lane = lax.broadcasted_iota(jnp.int32, (L, LANE), 1)
    v_ext = jnp.concatenate([v_ref[0], jnp.where(lane == 0, 1.0, 0.0).astype(BF16)], axis=1)
    c_prev = c_ref[...]
    num_ext = (jnp.dot(p, v_ext, preferred_element_type=F32)
               + w_inter * jnp.dot(q, c_prev.astype(BF16), preferred_element_type=F32))
    num = num_ext[:, :ML_DV]
    den = num_ext[:, ML_DV:ML_DV + 1]
    hh = num / jnp.maximum(jnp.abs(den), jnp.exp(-m_t))
    ms = jnp.mean(hh * hh, axis=1, keepdims=True)
    hn = hh * lax.rsqrt(ms + EPS) * gain_ref[...]
    out_ref[0] = (_sigmoid(og_ref[0].astype(F32)) * hn).astype(out_ref.dtype)

    g_tot = jnp.sum(lf, axis=1, keepdims=True)
    a = g_tot - b_s[0:1, :] + li
    m_loc = jnp.max(a, axis=1, keepdims=True)
    w_loc = jnp.exp(a - m_loc)
    k_t = (k.astype(F32) * scale).T
    c_loc = jnp.dot((k_t * w_loc).astype(BF16), v_ext, preferred_element_type=F32)
    m_new = jnp.maximum(g_tot + m_prev, m_loc)
    sp = jnp.exp(g_tot + m_prev - m_new)
    sl = jnp.exp(m_loc - m_new)
    c_ref[...] = sp * c_prev + sl * c_loc
    m_ref[...] = jnp.broadcast_to(m_new, m_ref.shape)


def mlstm_core(proj, ig, fg, b_i, b_f, h_gain, batch, tp):
    L = ML_CHUNK
    proj3 = proj.reshape(batch, tp, proj.shape[1])
    nqk = ML_HEADS
    nv = (2 * ML_HEADS * ML_DQK) // ML_DV
    smem = pl.BlockSpec(memory_space=pltpu.SMEM)
    out = pl.pallas_call(
        _mlstm_kernel,
        grid=(batch, ML_HEADS, tp // L),
        in_specs=[smem, smem,
                  pl.BlockSpec((1, L, ML_DQK), lambda b, h, c: (b, c, h)),
                  pl.BlockSpec((1, L, ML_DQK), lambda b, h, c: (b, c, nqk + h)),
                  pl.BlockSpec((1, L, ML_DV), lambda b, h, c: (b, c, nv + h)),
                  pl.BlockSpec((1, L, ML_DV), lambda b, h, c: (b, c, nv + ML_HEADS + h)),
                  pl.BlockSpec((1, 1, 1, L), lambda b, h, c: (b, h, 0, c)),
                  pl.BlockSpec((1, 1, 1, L), lambda b, h, c: (b, h, 0, c)),
                  pl.BlockSpec((1, ML_DV), lambda b, h, c: (0, h))],
        out_specs=pl.BlockSpec((1, L, ML_DV), lambda b, h, c: (b, c, h)),
        out_shape=jax.ShapeDtypeStruct((batch, tp, ML_HEADS * ML_DV), BF16),
        scratch_shapes=[pltpu.VMEM((ML_DQK, ML_DV + LANE), F32), pltpu.VMEM((8, LANE), F32)],
        compiler_params=_params(("parallel", "parallel", "arbitrary")),
        name="mlstm_core",
    )(b_i, b_f, proj3, proj3, proj3, proj3, ig, fg, h_gain.reshape(1, -1))
    return out.reshape(batch * tp, ML_HEADS * ML_DV)


def _fox_cumsum_kernel(fg_ref, bf_ref, o_ref, carry_ref):
    ci = pl.program_id(1)

    @pl.when(ci == 0)
    def _():
        carry_ref[...] = jnp.zeros_like(carry_ref)

    pos = ci * LANE + lax.broadcasted_iota(jnp.int32, (1, LANE), 1)
    x = jnp.where(pos < N_DUMMY, 0.0, _log_sigmoid(fg_ref[0] + bf_ref[...]))
    r = lax.broadcasted_iota(jnp.int32, (LANE, LANE), 0)
    c = lax.broadcasted_iota(jnp.int32, (LANE, LANE), 1)
    upper = jnp.where(r <= c, 1.0, 0.0).astype(BF16)
    x1, x2, x3 = _split3(x)
    cs = (jnp.dot(x1, upper, preferred_element_type=F32)
          + jnp.dot(x2, upper, preferred_element_type=F32)
          + jnp.dot(x3, upper, preferred_element_type=F32)) + carry_ref[...]
    o_ref[0] = cs
    carry_ref[...] = jnp.broadcast_to(cs[:, LANE - 1:LANE], carry_ref.shape)


def fox_cumsum(fg, b_f):
    b, h, tp = fg.shape
    return pl.pallas_call(
        _fox_cumsum_kernel,
        grid=(b, tp // LANE),
        in_specs=[pl.BlockSpec((1, h, LANE), lambda i, c: (i, 0, c)),
                  pl.BlockSpec((h, 1), lambda i, c: (0, 0))],
        out_specs=pl.BlockSpec((1, h, LANE), lambda i, c: (i, 0, c)),
        out_shape=jax.ShapeDtypeStruct((b, h, tp), F32),
        scratch_shapes=[pltpu.VMEM((h, LANE), F32)],
        compiler_params=_params(("parallel", "arbitrary")),
        name="fox_cumsum",
    )(fg, b_f.reshape(h, 1))


def _fox_attn_kernel(q_ref, k_ref, v_ref, og_ref, c_ref, o_ref, m_sc, l_sc, acc_sc, *, blk):
    qi = pl.program_id(2)
    ki = pl.program_id(3)

    @pl.when(ki == 0)
    def _():
        m_sc[...] = jnp.full_like(m_sc, -jnp.inf)
        l_sc[...] = jnp.zeros_like(l_sc)
        acc_sc[...] = jnp.zeros_like(acc_sc)

    @pl.when(ki <= qi)
    def _():
        q = q_ref[0]
        k = k_ref[0]
        v = v_ref[0]
        lo = lax.broadcasted_iota(jnp.int32, (1, LANE), 1) < FOX_DH
        c_rows = c_ref[0, 0]
        tpos = qi * blk + lax.broadcasted_iota(jnp.int32, (blk, 1), 0)
        spos = ki * blk + lax.broadcasted_iota(jnp.int32, (1, blk), 1)
        valid = (spos <= tpos) & ((spos >= N_DUMMY) | (tpos < N_DUMMY))
        alphas = []
        pv = None
        for hh in range(2):
            sel = lo if hh == 0 else jnp.logical_not(lo)
            qh = jnp.where(sel, q, jnp.zeros_like(q))
            s = lax.dot_general(qh, k, (((1,), (1,)), ((), ())), preferred_element_type=F32)
            s = jnp.where(valid, s - c_rows[hh:hh + 1, :], -jnp.inf)
            m_prev = m_sc[hh]
            m_new = jnp.maximum(m_prev, jnp.max(s, axis=1, keepdims=True))
            alpha = jnp.exp(m_prev - m_new)
            p = jnp.exp(s - m_new)
            l_sc[hh] = alpha * l_sc[hh] + jnp.sum(p, axis=1, keepdims=True)
            m_sc[hh] = m_new
            vh = jnp.where(sel, v, jnp.zeros_like(v))
            d = jnp.dot(p.astype(BF16), vh, preferred_element_type=F32)
            pv = d if pv is None else pv + d
            alphas.append(alpha)
        acc_sc[...] = acc_sc[...] * jnp.where(lo, alphas[0], alphas[1]) + pv

    @pl.when(ki == qi)
    def _():
        lo = lax.broadcasted_iota(jnp.int32, (1, LANE), 1) < FOX_DH
        y = acc_sc[...] / jnp.where(lo, l_sc[0], l_sc[1])
        o_ref[0] = (_sigmoid(og_ref[0].astype(F32)) * y).astype(o_ref.dtype)


def fox_attention(proj, c, batch, tp):
    d = FOX_HEADS * FOX_DH
    blk = _tile(tp, 512)
    nblk = tp // blk
    proj3 = proj.reshape(batch, tp, proj.shape[1])
    c4 = c.reshape(batch, FOX_PAIRS, 2, tp)
    out = pl.pallas_call(
        functools.partial(_fox_attn_kernel, blk=blk),
        grid=(batch, FOX_PAIRS, nblk, nblk),
        in_specs=[pl.BlockSpec((1, blk, LANE), lambda b, p, qi, ki: (b, qi, p)),
                  pl.BlockSpec((1, blk, LANE), lambda b, p, qi, ki: (b, jnp.minimum(ki, qi), FOX_PAIRS + p)),
                  pl.BlockSpec((1, blk, LANE), lambda b, p, qi, ki: (b, jnp.minimum(ki, qi), 2 * FOX_PAIRS + p)),
                  pl.BlockSpec((1, blk, LANE), lambda b, p, qi, ki: (b, qi, 3 * FOX_PAIRS + p)),
                  pl.BlockSpec((1, 1, 2, blk), lambda b, p, qi, ki: (b, p, 0, jnp.minimum(ki, qi)))],
        out_specs=pl.BlockSpec((1, blk, LANE), lambda b, p, qi, ki: (b, qi, p)),
        out_shape=jax.ShapeDtypeStruct((batch, tp, d), BF16),
        scratch_shapes=[pltpu.VMEM((2, blk, 1), F32), pltpu.VMEM((2, blk, 1), F32),
                        pltpu.VMEM((blk, LANE), F32)],
        compiler_params=_params(("parallel", "parallel", "parallel", "arbitrary")),
        name="fox_attention",
    )(proj3, proj3, proj3, proj3, c4)
    return out.reshape(batch * tp, d)


def _ffn_kernel(te_ref, nu_ref, x_ref, g_ref, wg_ref, wu_ref, wd_ref, *rest, has_resid):
    if has_resid:
        resid_ref, o_ref, xb_ref = rest
    else:
        (o_ref, xb_ref), resid_ref = rest, None
    i = pl.program_id(0)
    f = pl.program_id(1)

    @pl.when(i < nu_ref[0])
    def _():
        @pl.when(f == 0)
        def _():
            x = x_ref[...]
            ms = jnp.mean(x * x, axis=-1, keepdims=True)
            xb_ref[...] = (x * lax.rsqrt(ms + EPS) * g_ref[...]).astype(BF16)

        xb = xb_ref[...]
        gate = jnp.dot(xb, wg_ref[0].astype(BF16), preferred_element_type=F32)
        up = jnp.dot(xb, wu_ref[0].astype(BF16), preferred_element_type=F32)
        act = (gate * _sigmoid(gate) * up).astype(BF16)
        contrib = jnp.dot(act, wd_ref[0].astype(BF16), preferred_element_type=F32)

        @pl.when(f == 0)
        def _():
            if has_resid:
                o_ref[...] = resid_ref[...] + contrib
            else:
                o_ref[...] = contrib

        @pl.when(f > 0)
        def _():
            o_ref[...] += contrib

    @pl.when((i >= nu_ref[0]) & (f == 0))
    def _():
        o_ref[...] = jnp.zeros_like(o_ref)


def grouped_ffn(x, norm_g, w_gate, w_up, w_down, tile_expert, n_used, tm, tf, resid=None):
    r, d = x.shape
    ff = w_gate.shape[2]
    n_tiles = r // tm
    n_ff = ff // tf
    assert n_tiles * tm == r and n_ff * tf == ff

    def row_map(i, f, te, nu):
        return (jnp.minimum(i, nu[0] - 1), 0)

    def _ef(i, f, te, nu):
        ii = jnp.minimum(i, nu[0] - 1)
        return te[ii], jnp.where(i < nu[0], f, n_ff - 1)

    def wcol_map(i, f, te, nu):
        e, fi = _ef(i, f, te, nu)
        return (e, 0, fi)

    def wrow_map(i, f, te, nu):
        e, fi = _ef(i, f, te, nu)
        return (e, fi, 0)

    in_specs = [pl.BlockSpec((tm, d), row_map),
                pl.BlockSpec((1, d), lambda i, f, te, nu: (0, 0)),
                pl.BlockSpec((1, d, tf), wcol_map),
                pl.BlockSpec((1, d, tf), wcol_map),
                pl.BlockSpec((1, tf, d), wrow_map)]
    args = [x, norm_g.reshape(1, d), w_gate, w_up, w_down]
    if resid is not None:
        in_specs.append(pl.BlockSpec((tm, d), row_map))
        args.append(resid)
    return pl.pallas_call(
        functools.partial(_ffn_kernel, has_resid=resid is not None),
        grid_spec=pltpu.PrefetchScalarGridSpec(
            num_scalar_prefetch=2,
            grid=(n_tiles, n_ff),
            in_specs=in_specs,
            out_specs=pl.BlockSpec((tm, d), lambda i, f, te, nu: (i, 0)),
            scratch_shapes=[pltpu.VMEM((tm, d), BF16)]),
        out_shape=jax.ShapeDtypeStruct((r, d), F32),
        compiler_params=_params(("arbitrary", "arbitrary")),
        name="grouped_ffn",
    )(tile_expert, n_used, *args)


def _router_kernel(x_ref, g_ref, rt_ref, idx_ref, gate_ref, rank_ref, cnt_ref, base_ref, *, tm):
    i = pl.program_id(0)

    @pl.when(i == 0)
    def _():
        base_ref[...] = jnp.zeros_like(base_ref)

    x = x_ref[...]
    ms = jnp.mean(x * x, axis=-1, keepdims=True)
    hn = x * lax.rsqrt(ms + EPS) * g_ref[...]
    r1, r2, _ = _split3(rt_ref[...])
    h1, h2, _ = _split3(hn)
    nt = (((1,), (1,)), ((), ()))
    logits = (lax.dot_general(r1, h1, nt, preferred_element_type=F32)
              + lax.dot_general(r1, h2, nt, preferred_element_type=F32)
              + lax.dot_general(r2, h1, nt, preferred_element_type=F32))

    e_i = lax.broadcasted_iota(jnp.int32, (N_EXPERTS, tm), 0)
    v1 = jnp.max(logits, axis=0, keepdims=True)
    i1 = jnp.min(jnp.where(logits == v1, e_i, N_EXPERTS), axis=0, keepdims=True)
    sel1 = e_i == i1
    rest = jnp.where(sel1, -jnp.inf, logits)
    v2 = jnp.max(rest, axis=0, keepdims=True)
    i2 = jnp.min(jnp.where(rest == v2, e_i, N_EXPERTS), axis=0, keepdims=True)
    sel2 = e_i == i2
    e2 = jnp.exp(v2 - v1)
    den = 1.0 + e2
    idx_ref[...] = jnp.concatenate([i1, i2], axis=0)
    gate_ref[...] = jnp.concatenate([1.0 / den, e2 / den], axis=0)

    sel = jnp.where(sel1 | sel2, 1.0, 0.0)
    t_r = lax.broadcasted_iota(jnp.int32, (tm, tm), 0)
    t_c = lax.broadcasted_iota(jnp.int32, (tm, tm), 1)
    before = jnp.where(t_r < t_c, 1.0, 0.0).astype(BF16)
    tot = base_ref[:, 0:1] + jnp.dot(sel.astype(BF16), before, preferred_element_type=F32)
    rk1 = jnp.sum(jnp.where(sel1, tot, 0.0), axis=0, keepdims=True)
    rk2 = jnp.sum(jnp.where(sel2, tot, 0.0), axis=0, keepdims=True)
    rank_ref[...] = jnp.concatenate([rk1, rk2], axis=0).astype(jnp.int32)
    new_base = base_ref[...] + jnp.sum(sel, axis=1, keepdims=True)
    base_ref[...] = new_base
    cnt_ref[...] = new_base.astype(jnp.int32)


def router(h, norm_g, router_w):
    n, d = h.shape
    tm = _tile(n, 512)
    row2 = lambda i: (0, i)
    idx, gate, rank, cnt = pl.pallas_call(
        functools.partial(_router_kernel, tm=tm),
        grid=(n // tm,),
        in_specs=[pl.BlockSpec((tm, d), lambda i: (i, 0)),
                  pl.BlockSpec((1, d), lambda i: (0, 0)),
                  pl.BlockSpec((N_EXPERTS, d), lambda i: (0, 0))],
        out_specs=[pl.BlockSpec((TOP_K, tm), row2), pl.BlockSpec((TOP_K, tm), row2),
                   pl.BlockSpec((TOP_K, tm), row2), pl.BlockSpec((N_EXPERTS, LANE), lambda i: (0, 0))],
        out_shape=[jax.ShapeDtypeStruct((TOP_K, n), jnp.int32), jax.ShapeDtypeStruct((TOP_K, n), F32),
                   jax.ShapeDtypeStruct((TOP_K, n), jnp.int32), jax.ShapeDtypeStruct((N_EXPERTS, LANE), jnp.int32)],
        scratch_shapes=[pltpu.VMEM((N_EXPERTS, LANE), F32)],
        compiler_params=_params(("arbitrary",)),
        name="router",
    )(h, norm_g.reshape(1, d), router_w.T)
    return idx, gate, rank, cnt[:, 0]


def _row_copy(src_ref, s, dst_ref, t, sem):
    return pltpu.make_async_copy(src_ref.at[pl.ds(s, 1), :], dst_ref.at[pl.ds(t, 1), :], sem)


def _dispatch_kernel(pos_ref, h_ref, xs_in_ref, xs_ref, sem, *, tt):
    del xs_in_ref
    base = pl.program_id(0) * tt

    def start(j, carry):
        for kk in range(TOP_K):
            _row_copy(h_ref, base + j, xs_ref, pos_ref[0, kk, j], sem).start()
        return carry

    def wait(j, carry):
        for kk in range(TOP_K):
            _row_copy(h_ref, base + j, xs_ref, pos_ref[0, kk, j], sem).wait()
        return carry

    lax.fori_loop(0, tt, start, 0)
    lax.fori_loop(0, tt, wait, 0)


def dispatch(h, pos_tiles, n_rows, tt):
    n, d = h.shape
    xs0 = jnp.zeros((n_rows, d), h.dtype)
    any_spec = pl.BlockSpec(memory_space=pl.ANY)
    return pl.pallas_call(
        functools.partial(_dispatch_kernel, tt=tt),
        grid=(n // tt,),
        in_specs=[pl.BlockSpec((1, TOP_K, tt), lambda i: (i, 0, 0), memory_space=pltpu.SMEM),
                  any_spec, any_spec],
        out_specs=any_spec,
        out_shape=jax.ShapeDtypeStruct((n_rows, d), h.dtype),
        scratch_shapes=[pltpu.SemaphoreType.DMA],
        input_output_aliases={2: 0},
        compiler_params=_params(("arbitrary",)),
        name="moe_dispatch",
    )(pos_tiles, h, xs0)


def _combine_kernel(pos_ref, h_ref, gate_ref, ys_ref, o_ref, buf_ref, sem, *, tt):
    def start(j, carry):
        for kk in range(TOP_K):
            _row_copy(ys_ref, pos_ref[0, kk, j], buf_ref.at[kk], j, sem).start()
        return carry

    def wait(j, carry):
        for kk in range(TOP_K):
            _row_copy(ys_ref, pos_ref[0, kk, j], buf_ref.at[kk], j, sem).wait()
        return carry

    lax.fori_loop(0, tt, start, 0)
    lax.fori_loop(0, tt, wait, 0)
    g = gate_ref[...]
    o_ref[...] = h_ref[...] + (g[:, 0:1] * buf_ref[0] + g[:, 1:2] * buf_ref[1])


def combine(h, gate_cols, ys, pos_tiles, tt):
    n, d = h.shape
    return pl.pallas_call(
        functools.partial(_combine_kernel, tt=tt),
        grid=(n // tt,),
        in_specs=[pl.BlockSpec((1, TOP_K, tt), lambda i: (i, 0, 0), memory_space=pltpu.SMEM),
                  pl.BlockSpec((tt, d), lambda i: (i, 0)),
                  pl.BlockSpec((tt, TOP_K), lambda i: (i, 0)),
                  pl.BlockSpec(memory_space=pl.ANY)],
        out_specs=pl.BlockSpec((tt, d), lambda i: (i, 0)),
        out_shape=jax.ShapeDtypeStruct((n, d), F32),
        scratch_shapes=[pltpu.VMEM((TOP_K, tt, d), F32), pltpu.SemaphoreType.DMA],
        compiler_params=_params(("arbitrary",)),
        name="moe_combine",
    )(pos_tiles, h, gate_cols, ys)


def moe_layer(h, norm_g, router_w, w_gate, w_up, w_down, tm=512, tf=512, tt=256):
    n, d = h.shape
    idx, gate, rank, counts = router(h, norm_g, router_w)
    tiles_e = (counts + tm - 1) // tm
    tile_end = jnp.cumsum(tiles_e)
    starts = (tile_end - tiles_e) * tm
    pos = starts[idx] + rank
    n_tiles = (TOP_K * n) // tm + N_EXPERTS
    tile_expert = jnp.minimum(jnp.searchsorted(tile_end, jnp.arange(n_tiles, dtype=jnp.int32), side="right"),
                              N_EXPERTS - 1).astype(jnp.int32)
    n_used = tile_end[-1:].astype(jnp.int32)
    tt = _tile(n, tt)
    pos_tiles = pos.reshape(TOP_K, n // tt, tt).transpose(1, 0, 2)
    xs = dispatch(h, pos_tiles, n_tiles * tm, tt)
    ys = grouped_ffn(xs, norm_g, w_gate, w_up, w_down, tile_expert, n_used, tm, tf)
    return combine(h, gate.T, ys, pos_tiles, tt)


def dense_ffn(h, norm_g, w_gate, w_up, w_down, tm=512, tf=512):
    n, d = h.shape
    tm = _tile(n, tm)
    n_tiles = n // tm
    return grouped_ffn(h, norm_g, w_gate[None], w_up[None], w_down[None],
                       jnp.zeros((n_tiles,), jnp.int32), jnp.full((1,), n_tiles, jnp.int32), tm, tf, resid=h)


def mlstm_layer(h, batch, tp, norm_g, w_in, b_i, b_f, h_gain, w_out):
    d = h.shape[1]
    n_main = 2 * ML_HEADS * ML_DQK + 2 * ML_HEADS * ML_DV
    hn = rmsnorm(h, norm_g, BF16)
    proj = matmul(hn, w_in.astype(BF16), n_main, BF16)
    gates = gate_proj(hn, w_in[:, n_main:].T.astype(BF16))
    gates = gates.reshape(2, ML_HEADS, batch, tp).transpose(0, 2, 1, 3).reshape(2, batch, ML_HEADS, 1, tp)
    y = mlstm_core(proj, gates[0], gates[1], b_i, b_f, h_gain, batch, tp)
    return matmul(y, w_out.astype(BF16), d, F32, resid=h)


def fox_layer(h, batch, tp, norm_g, w_in, b_f, q_gain, k_gain, w_out):
    d = h.shape[1]
    n_main = 4 * d
    hn = rmsnorm(h, norm_g, BF16)
    head_gain = jnp.concatenate([jnp.tile(q_gain * (FOX_DH ** -0.5), FOX_HEADS), jnp.tile(k_gain, FOX_HEADS),
                                 jnp.ones((2 * d,), F32)]).reshape(1, n_main)
    proj = matmul(hn, w_in.astype(BF16), n_main, BF16, head_gain=head_gain, n_norm_cols=2 * d)
    fg = gate_proj(hn, w_in[:, n_main:].T.astype(BF16))
    fg = fg.reshape(FOX_HEADS, batch, tp).transpose(1, 0, 2)
    c = fox_cumsum(fg, b_f)
    y = fox_attention(proj, c, batch, tp)
    return matmul(y, w_out.astype(BF16), d, F32, resid=h)


def kernel(x, meta_tokens, ml_norm, ml_w_in, ml_b_i, ml_b_f, ml_h_gain, ml_w_out, ffn_norm, ffn_w_gate, ffn_w_up, ffn_w_down, fox_norm, fox_w_in, fox_b_f, fox_q_gain, fox_k_gain, fox_w_out, moe_norm, moe_router, moe_w_gate, moe_w_up, moe_w_down, final_norm):
    batch, seq, d = x.shape
    tp = FRONT + seq
    depth = ml_norm.shape[0] + fox_norm.shape[0]
    front = jnp.concatenate([jnp.zeros((N_DUMMY, d), x.dtype), meta_tokens.astype(x.dtype)], axis=0)
    h = jnp.concatenate([jnp.broadcast_to(front[None], (batch, FRONT, d)), x], axis=1).reshape(batch * tp, d)
    bf = lambda w: w.astype(BF16)
    for i in range(depth):
        j = i // 2
        if i % 2 == 0:
            h = mlstm_layer(h, batch, tp, ml_norm[j], ml_w_in[j], ml_b_i[j], ml_b_f[j], ml_h_gain[j], ml_w_out[j])
            h = dense_ffn(h, ffn_norm[j], bf(ffn_w_gate[j]), bf(ffn_w_up[j]), bf(ffn_w_down[j]))
        else:
            h = fox_layer(h, batch, tp, fox_norm[j], fox_w_in[j], fox_b_f[j], fox_q_gain[j], fox_k_gain[j], fox_w_out[j])
            h = moe_layer(h, moe_norm[j], moe_router[j], bf(moe_w_gate[j]), bf(moe_w_up[j]), bf(moe_w_down[j]))
    return final_rmsnorm(h.reshape(batch, tp, d), final_norm)
```

```python
import functools

import jax
import jax.numpy as jnp
from jax import lax
from jax.experimental import pallas as pl
from jax.experimental.pallas import tpu as pltpu

F32 = jnp.float32
BF16 = jnp.bfloat16

N_META = 16
EPS = 1e-6
LANE = 128
FRONT = 128
N_DUMMY = FRONT - N_META
ML_HEADS = 8
ML_DQK = 128
ML_DV = 256
ML_CHUNK = 128
GATE_CAP = 15.0
NEG_BIG = -1e30
FOX_HEADS = 32
FOX_DH = 64
FOX_PAIRS = FOX_HEADS // 2
FOX_ROWS = 16
LOG2E = 1.4426950408889634
N_EXPERTS = 8
TOP_K = 2

VMEM_LIMIT = 56 * 1024 * 1024


def _params(sem, vmem=VMEM_LIMIT):
    return pltpu.CompilerParams(dimension_semantics=sem, vmem_limit_bytes=vmem)


def _tile(n, pref):
    best = None
    t = LANE
    while t <= min(n, pref):
        if n % t == 0:
            best = t
        t += LANE
    assert best is not None, (n, pref)
    return best


def _split3(x):
    x1 = x.astype(BF16)
    r1 = x - x1.astype(F32)
    x2 = r1.astype(BF16)
    r2 = r1 - x2.astype(F32)
    return x1, x2, r2.astype(BF16)


def _log_sigmoid(x):
    return -(jnp.maximum(-x, 0.0) + jnp.log1p(jnp.exp(-jnp.abs(x))))


def _sigmoid(x):
    return 1.0 / (1.0 + jnp.exp(-x))


def _rmsnorm_kernel(x_ref, g_ref, o_ref):
    x = x_ref[...]
    ms = jnp.mean(x * x, axis=-1, keepdims=True)
    o_ref[...] = (x * lax.rsqrt(ms + EPS) * g_ref[...]).astype(o_ref.dtype)


def rmsnorm(x, g, out_dtype):
    n, d = x.shape
    tm = _tile(n, 512)
    return pl.pallas_call(
        _rmsnorm_kernel,
        grid=(n // tm,),
        in_specs=[pl.BlockSpec((tm, d), lambda i: (i, 0)),
                  pl.BlockSpec((1, d), lambda i: (0, 0))],
        out_specs=pl.BlockSpec((tm, d), lambda i: (i, 0)),
        out_shape=jax.ShapeDtypeStruct((n, d), out_dtype),
        compiler_params=_params(("parallel",)),
        name="rmsnorm",
    )(x, g.reshape(1, d))


def _final_norm_kernel(x_ref, g_ref, o_ref):
    x = x_ref[0]
    ms = jnp.mean(x * x, axis=-1, keepdims=True)
    o_ref[0] = x * lax.rsqrt(ms + EPS) * g_ref[...]


def final_rmsnorm(h3, g):
    b, tp, d = h3.shape
    nblk = (tp - FRONT) // LANE
    return pl.pallas_call(
        _final_norm_kernel,
        grid=(b, nblk),
        in_specs=[pl.BlockSpec((1, LANE, d), lambda i, j: (i, j + FRONT // LANE, 0)),
                  pl.BlockSpec((1, d), lambda i, j: (0, 0))],
        out_specs=pl.BlockSpec((1, LANE, d), lambda i, j: (i, j, 0)),
        out_shape=jax.ShapeDtypeStruct((b, tp - FRONT, d), F32),
        compiler_params=_params(("parallel", "parallel")),
        name="final_norm",
    )(h3, g.reshape(1, d))


def _group_sum_matrix():
    r = lax.broadcasted_iota(jnp.int32, (LANE, LANE), 0) // FOX_DH
    c = lax.broadcasted_iota(jnp.int32, (LANE, LANE), 1) // FOX_DH
    return jnp.where(r == c, 1.0, 0.0).astype(BF16)


def _matmul_kernel(*refs, has_resid, n_norm_tiles):
    x_ref, w_ref = refs[0], refs[1]
    k = 2
    gain_ref = resid_ref = None
    if n_norm_tiles:
        gain_ref = refs[k]; k += 1
    if has_resid:
        resid_ref = refs[k]; k += 1
    o_ref = refs[k]
    acc = jnp.dot(x_ref[...], w_ref[...], preferred_element_type=F32)
    if has_resid:
        acc = resid_ref[...] + acc

    if not n_norm_tiles:
        o_ref[...] = acc.astype(o_ref.dtype)
        return

    j = pl.program_id(0)

    @pl.when(j >= n_norm_tiles)
    def _():
        o_ref[...] = acc.astype(o_ref.dtype)

    @pl.when(j < n_norm_tiles)
    def _():
        gmat = _group_sum_matrix()
        tn = acc.shape[1]
        for c in range(tn // LANE):
            y = acc[:, c * LANE:(c + 1) * LANE]
            y2 = y * y
            hi = y2.astype(BF16)
            lo = (y2 - hi.astype(F32)).astype(BF16)
            ss = (jnp.dot(hi, gmat, preferred_element_type=F32)
                  + jnp.dot(lo, gmat, preferred_element_type=F32))
            yn = y * lax.rsqrt(ss * (1.0 / FOX_DH) + EPS) * gain_ref[:, c * LANE:(c + 1) * LANE]
            o_ref[:, c * LANE:(c + 1) * LANE] = yn.astype(o_ref.dtype)


def matmul(x, w, n_cols, out_dtype, resid=None, head_gain=None, n_norm_cols=0, tm_pref=512, tn_pref=1024):
    m, kdim = x.shape
    tm = _tile(m, tm_pref)
    tn = _tile(n_cols, tn_pref)
    n_norm_tiles = n_norm_cols // tn
    assert n_norm_tiles * tn == n_norm_cols
    in_specs = [pl.BlockSpec((tm, kdim), lambda j, i: (i, 0)),
                pl.BlockSpec((kdim, tn), lambda j, i: (0, j))]
    args = [x, w]
    if n_norm_tiles:
        in_specs.append(pl.BlockSpec((1, tn), lambda j, i: (0, j)))
        args.append(head_gain)
    if resid is not None:
        in_specs.append(pl.BlockSpec((tm, tn), lambda j, i: (i, j)))
        args.append(resid)
    return pl.pallas_call(
        functools.partial(_matmul_kernel, has_resid=resid is not None, n_norm_tiles=n_norm_tiles),
        grid=(n_cols // tn, m // tm),
        in_specs=in_specs,
        out_specs=pl.BlockSpec((tm, tn), lambda j, i: (i, j)),
        out_shape=jax.ShapeDtypeStruct((m, n_cols), out_dtype),
        compiler_params=_params(("parallel", "parallel")),
        name="matmul",
    )(*args)


def _gate_proj_kernel(w_ref, x_ref, o_ref):
    o_ref[...] = lax.dot_general(w_ref[...], x_ref[...], (((1,), (1,)), ((), ())),
                                 preferred_element_type=F32)


def gate_proj(x, w_t):
    m, kdim = x.shape
    g = w_t.shape[0]
    tm = _tile(m, 512)
    return pl.pallas_call(
        _gate_proj_kernel,
        grid=(m // tm,),
        in_specs=[pl.BlockSpec((g, kdim), lambda i: (0, 0)),
                  pl.BlockSpec((tm, kdim), lambda i: (i, 0))],
        out_specs=pl.BlockSpec((g, tm), lambda i: (0, i)),
        out_shape=jax.ShapeDtypeStruct((g, m), F32),
        compiler_params=_params(("parallel",)),
        name="gate_proj",
    )(w_t, x)


def _mlstm_kernel(bi_ref, bf_ref, q_ref, k_ref, v_ref, og_ref, ig_ref, fg_ref, gain_ref,
                  out_ref, c_ref, m_ref):
    L = ML_CHUNK
    hd = pl.program_id(1)
    ci = pl.program_id(2)

    @pl.when(ci == 0)
    def _():
        c_ref[...] = jnp.zeros_like(c_ref)
        m_ref[...] = jnp.zeros_like(m_ref)

    pos = ci * L + lax.broadcasted_iota(jnp.int32, (1, L), 1)
    dummy = pos < N_DUMMY
    ig = ig_ref[0, 0] + bi_ref[hd]
    fg = fg_ref[0, 0] + bf_ref[hd]
    li = jnp.where(dummy, NEG_BIG, GATE_CAP * jnp.tanh(ig / GATE_CAP))
    lf = jnp.where(dummy, 0.0, _log_sigmoid(GATE_CAP * jnp.tanh(fg / GATE_CAP)))

    t_i = lax.broadcasted_iota(jnp.int32, (L, L), 0)
    s_i = lax.broadcasted_iota(jnp.int32, (L, L), 1)
    tril = s_i <= t_i
    b_col = jnp.sum(jnp.where(tril, jnp.broadcast_to(lf, (L, L)), 0.0), axis=1, keepdims=True)
    b_t = jnp.broadcast_to(b_col, (L, L))
    b_s = b_t.T
    d_log = jnp.where(tril, b_t - b_s + li, -jnp.inf)
    m_prev = m_ref[0:1, 0:1]
    inter = b_col + m_prev
    m_t = jnp.maximum(inter, jnp.max(d_log, axis=1, keepdims=True))

    q = q_ref[0]
    k = k_ref[0]
    scale = ML_DQK ** -0.5
    s = lax.dot_general(q, k, (((1,), (1,)), ((), ())), preferred_element_type=F32) * scale
    p = (s * jnp.exp(d_log - m_t)).astype(BF16)
    w_inter = jnp.exp(inter - m_t)

    lane = lax.broadcasted_iota(jnp.int32, (L, LANE), 1)
    v_ext = jnp.concatenate([v_ref[0], jnp.where(lane == 0, 1.0, 0.0).astype(BF16)], axis=1)
    c_prev = c_ref[...]
    num_ext = (jnp.dot(p, v_ext, preferred_element_type=F32)
               + w_inter * jnp.dot(q, c_prev.astype(BF16), preferred_element_type=F32))
    num = num_ext[:, :ML_DV]
    den = num_ext[:, ML_DV:ML_DV + 1]
    hh = num / jnp.maximum(jnp.abs(den), jnp.exp(-m_t))
    ms = jnp.mean(hh * hh, axis=1, keepdims=True)
    hn = hh * lax.rsqrt(ms + EPS) * gain_ref[...]
    out_ref[0] = (_sigmoid(og_ref[0].astype(F32)) * hn).astype(out_ref.dtype)

    g_tot = jnp.sum(lf, axis=1, keepdims=True)
    a = g_tot - b_s[0:1, :] + li
    m_loc = jnp.max(a, axis=1, keepdims=True)
    w_loc = jnp.exp(a - m_loc)
    k_t = (k.astype(F32) * scale).T
    c_loc = jnp.dot((k_t * w_loc).astype(BF16), v_ext, preferred_element_type=F32)
    m_new = jnp.maximum(g_tot + m_prev, m_loc)
    sp = jnp.exp(g_tot + m_prev - m_new)
    sl = jnp.exp(m_loc - m_new)
    c_ref[...] = sp * c_prev + sl * c_loc
    m_ref[...] = jnp.broadcast_to(m_new, m_ref.shape)


def mlstm_core(proj, ig, fg, b_i, b_f, h_gain, batch, tp):
    L = ML_CHUNK
    proj3 = proj.reshape(batch, tp, proj.shape[1])
    nqk = ML_HEADS
    nv = (2 * ML_HEADS * ML_DQK) // ML_DV
    smem = pl.BlockSpec(memory_space=pltpu.SMEM)
    out = pl.pallas_call(
        _mlstm_kernel,
        grid=(batch, ML_HEADS, tp // L),
        in_specs=[smem, smem,
                  pl.BlockSpec((1, L, ML_DQK), lambda b, h, c: (b, c, h)),
                  pl.BlockSpec((1, L, ML_DQK), lambda b, h, c: (b, c, nqk + h)),
                  pl.BlockSpec((1, L, ML_DV), lambda b, h, c: (b, c, nv + h)),
                  pl.BlockSpec((1, L, ML_DV), lambda b, h, c: (b, c, nv + ML_HEADS + h)),
                  pl.BlockSpec((1, 1, 1, L), lambda b, h, c: (b, h, 0, c)),
                  pl.BlockSpec((1, 1, 1, L), lambda b, h, c: (b, h, 0, c)),
                  pl.BlockSpec((1, ML_DV), lambda b, h, c: (0, h))],
        out_specs=pl.BlockSpec((1, L, ML_DV), lambda b, h, c: (b, c, h)),
        out_shape=jax.ShapeDtypeStruct((batch, tp, ML_HEADS * ML_DV), BF16),
        scratch_shapes=[pltpu.VMEM((ML_DQK, ML_DV + LANE), F32), pltpu.VMEM((8, LANE), F32)],
        compiler_params=_params(("parallel", "parallel", "arbitrary")),
        name="mlstm_core",
    )(b_i, b_f, proj3, proj3, proj3, proj3, ig, fg, h_gain.reshape(1, -1))
    return out.reshape(batch * tp, ML_HEADS * ML_DV)


def _fox_cumsum_kernel(fg_ref, bf_ref, o_ref, carry_ref):
    ci = pl.program_id(1)

    @pl.when(ci == 0)
    def _():
        carry_ref[...] = jnp.zeros_like(carry_ref)

    pos = ci * LANE + lax.broadcasted_iota(jnp.int32, (1, LANE), 1)
    x = jnp.where(pos < N_DUMMY, 0.0, LOG2E * _log_sigmoid(fg_ref[0] + bf_ref[...]))
    r = lax.broadcasted_iota(jnp.int32, (LANE, LANE), 0)
    c = lax.broadcasted_iota(jnp.int32, (LANE, LANE), 1)
    upper = jnp.where(r <= c, 1.0, 0.0).astype(BF16)
    x1, x2, x3 = _split3(x)
    cs = (jnp.dot(x1, upper, preferred_element_type=F32)
          + jnp.dot(x2, upper, preferred_element_type=F32)
          + jnp.dot(x3, upper, preferred_element_type=F32)) + carry_ref[...]
    o_ref[0] = cs
    carry_ref[...] = jnp.broadcast_to(cs[:, LANE - 1:LANE], carry_ref.shape)


def fox_cumsum(fg, b_f):
    b, h, tp = fg.shape
    return pl.pallas_call(
        _fox_cumsum_kernel,
        grid=(b, tp // LANE),
        in_specs=[pl.BlockSpec((1, h, LANE), lambda i, c: (i, 0, c)),
                  pl.BlockSpec((h, 1), lambda i, c: (0, 0))],
        out_specs=pl.BlockSpec((1, h, LANE), lambda i, c: (i, 0, c)),
        out_shape=jax.ShapeDtypeStruct((b, h, tp), F32),
        scratch_shapes=[pltpu.VMEM((h, LANE), F32)],
        compiler_params=_params(("parallel", "arbitrary")),
        name="fox_cumsum",
    )(fg, b_f.reshape(h, 1))


def _fox_attn_kernel(q_ref, k_ref, v_ref, og_ref, c_ref, o_ref,
                     qm_ref, s_ref, p_ref, m_ref, l_ref, a_ref, acc_ref, *, blk):
    qi = pl.program_id(2)
    nt = (((1,), (1,)), ((), ()))
    lo = lax.broadcasted_iota(jnp.int32, (1, LANE), 1) < FOX_DH
    q = q_ref[0]
    qm_ref[0] = jnp.where(lo, q, jnp.zeros_like(q))
    qm_ref[1] = jnp.where(lo, jnp.zeros_like(q), q)
    m_ref[...] = jnp.full_like(m_ref, -jnp.inf)
    l_ref[...] = jnp.zeros_like(l_ref)
    acc_ref[...] = jnp.zeros_like(acc_ref)

    def process(ki, diagonal):
        off = pl.multiple_of(ki * blk, blk)
        k = k_ref[0, pl.ds(off, blk), :]
        v = v_ref[0, pl.ds(off, blk), :]
        c_rows = c_ref[0, 0, ki]
        spos = off + lax.broadcasted_iota(jnp.int32, (1, blk), 1)
        if not diagonal:
            c_rows = jnp.where(spos < N_DUMMY, jnp.inf, c_rows)
        if diagonal:
            tpos = off + lax.broadcasted_iota(jnp.int32, (blk, 1), 0)
            valid = (spos <= tpos) & ((spos >= N_DUMMY) | (tpos < N_DUMMY))
        alphas, pvs = [], []
        for hh in range(2):
            s = lax.dot_general(qm_ref[hh], k, nt, preferred_element_type=F32) - c_rows[hh:hh + 1, :]
            if diagonal:
                s = jnp.where(valid, s, -jnp.inf)
            m_prev = m_ref[hh]
            m_new = jnp.maximum(m_prev, jnp.max(s, axis=1, keepdims=True))
            alpha = jnp.exp2(m_prev - m_new)
            p = jnp.exp2(s - m_new)
            l_ref[hh] = alpha * l_ref[hh] + jnp.sum(p, axis=1, keepdims=True)
            m_ref[hh] = m_new
            alphas.append(alpha)
            pvs.append(jnp.dot(p.astype(BF16), v, preferred_element_type=F32))
        acc = acc_ref[...]
        acc_ref[...] = jnp.where(lo, acc * alphas[0] + pvs[0], acc * alphas[1] + pvs[1])

    def body(ki, carry):
        process(ki, False)
        return carry

    lax.fori_loop(0, qi, body, 0)
    process(qi, True)
    y = acc_ref[...] / jnp.where(lo, l_ref[0], l_ref[1])
    o_ref[0] = (_sigmoid(og_ref[0].astype(F32)) * y).astype(o_ref.dtype)


def fox_attention(proj, c, batch, tp):
    d = FOX_HEADS * FOX_DH
    blk = _tile(tp, 512)
    nblk = tp // blk
    proj3 = proj.reshape(batch, tp, proj.shape[1])
    c5 = c.reshape(batch, FOX_PAIRS, 2, nblk, blk).transpose(0, 1, 3, 2, 4)
    out = pl.pallas_call(
        functools.partial(_fox_attn_kernel, blk=blk),
        grid=(batch, FOX_PAIRS, nblk),
        in_specs=[pl.BlockSpec((1, blk, LANE), lambda b, p, qi: (b, qi, p)),
                  pl.BlockSpec((1, tp, LANE), lambda b, p, qi: (b, 0, FOX_PAIRS + p)),
                  pl.BlockSpec((1, tp, LANE), lambda b, p, qi: (b, 0, 2 * FOX_PAIRS + p)),
                  pl.BlockSpec((1, blk, LANE), lambda b, p, qi: (b, qi, 3 * FOX_PAIRS + p)),
                  pl.BlockSpec((1, 1, nblk, 2, blk), lambda b, p, qi: (b, p, 0, 0, 0))],
        out_specs=pl.BlockSpec((1, blk, LANE), lambda b, p, qi: (b, qi, p)),
        out_shape=jax.ShapeDtypeStruct((batch, tp, d), BF16),
        scratch_shapes=[pltpu.VMEM((2, blk, LANE), BF16),
                        pltpu.VMEM((2, blk, blk), F32), pltpu.VMEM((2, blk, blk), BF16),
                        pltpu.VMEM((2, blk, 1), F32), pltpu.VMEM((2, blk, 1), F32), pltpu.VMEM((2, blk, 1), F32),
                        pltpu.VMEM((blk, LANE), F32)],
        compiler_params=_params(("parallel", "parallel", "arbitrary")),
        name="fox_attention",
    )(proj3, proj3, proj3, proj3, c5)
    return out.reshape(batch * tp, d)


def _ffn_kernel(te_ref, nu_ref, x_ref, g_ref, wg_ref, wu_ref, wd_ref, *rest, has_resid):
    if has_resid:
        resid_ref, o_ref, xb_ref = rest
    else:
        (o_ref, xb_ref), resid_ref = rest, None
    i = pl.program_id(0)
    f = pl.program_id(1)

    @pl.when(i < nu_ref[0])
    def _():
        @pl.when(f == 0)
        def _():
            x = x_ref[...]
            ms = jnp.mean(x * x, axis=-1, keepdims=True)
            xb_ref[...] = (x * lax.rsqrt(ms + EPS) * g_ref[...]).astype(BF16)

        xb = xb_ref[...]
        gate = jnp.dot(xb, wg_ref[0].astype(BF16), preferred_element_type=F32)
        up = jnp.dot(xb, wu_ref[0].astype(BF16), preferred_element_type=F32)
        act = (gate * _sigmoid(gate) * up).astype(BF16)
        contrib = jnp.dot(act, wd_ref[0].astype(BF16), preferred_element_type=F32)

        @pl.when(f == 0)
        def _():
            if has_resid:
                o_ref[...] = resid_ref[...] + contrib
            else:
                o_ref[...] = contrib

        @pl.when(f > 0)
        def _():
            o_ref[...] += contrib

    @pl.when((i >= nu_ref[0]) & (f == 0))
    def _():
        o_ref[...] = jnp.zeros_like(o_ref)


def grouped_ffn(x, norm_g, w_gate, w_up, w_down, tile_expert, n_used, tm, tf, resid=None):
    r, d = x.shape
    ff = w_gate.shape[2]
    n_tiles = r // tm
    n_ff = ff // tf
    assert n_tiles * tm == r and n_ff * tf == ff

    def row_map(i, f, te, nu):
        return (jnp.minimum(i, nu[0] - 1), 0)

    def _ef(i, f, te, nu):
        ii = jnp.minimum(i, nu[0] - 1)
        return te[ii], jnp.where(i < nu[0], f, n_ff - 1)

    def wcol_map(i, f, te, nu):
        e, fi = _ef(i, f, te, nu)
        return (e, 0, fi)

    def wrow_map(i, f, te, nu):
        e, fi = _ef(i, f, te, nu)
        return (e, fi, 0)

    in_specs = [pl.BlockSpec((tm, d), row_map),
                pl.BlockSpec((1, d), lambda i, f, te, nu: (0, 0)),
                pl.BlockSpec((1, d, tf), wcol_map),
                pl.BlockSpec((1, d, tf), wcol_map),
                pl.BlockSpec((1, tf, d), wrow_map)]
    args = [x, norm_g.reshape(1, d), w_gate, w_up, w_down]
    if resid is not None:
        in_specs.append(pl.BlockSpec((tm, d), row_map))
        args.append(resid)
    return pl.pallas_call(
        functools.partial(_ffn_kernel, has_resid=resid is not None),
        grid_spec=pltpu.PrefetchScalarGridSpec(
            num_scalar_prefetch=2,
            grid=(n_tiles, n_ff),
            in_specs=in_specs,
            out_specs=pl.BlockSpec((tm, d), lambda i, f, te, nu: (i, 0)),
            scratch_shapes=[pltpu.VMEM((tm, d), BF16)]),
        out_shape=jax.ShapeDtypeStruct((r, d), F32),
        compiler_params=_params(("arbitrary", "arbitrary")),
        name="grouped_ffn",
    )(tile_expert, n_used, *args)


def _router_kernel(x_ref, g_ref, rt_ref, idx_ref, gate_ref, rank_ref, cnt_ref, base_ref, *, tm):
    i = pl.program_id(0)

    @pl.when(i == 0)
    def _():
        base_ref[...] = jnp.zeros_like(base_ref)

    x = x_ref[...]
    ms = jnp.mean(x * x, axis=-1, keepdims=True)
    hn = x * lax.rsqrt(ms + EPS) * g_ref[...]
    r1, r2, _ = _split3(rt_ref[...])
    h1, h2, _ = _split3(hn)
    nt = (((1,), (1,)), ((), ()))
    logits = (lax.dot_general(r1, h1, nt, preferred_element_type=F32)
              + lax.dot_general(r1, h2, nt, preferred_element_type=F32)
              + lax.dot_general(r2, h1, nt, preferred_element_type=F32))

    e_i = lax.broadcasted_iota(jnp.int32, (N_EXPERTS, tm), 0)
    v1 = jnp.max(logits, axis=0, keepdims=True)
    i1 = jnp.min(jnp.where(logits == v1, e_i, N_EXPERTS), axis=0, keepdims=True)
    sel1 = e_i == i1
    rest = jnp.where(sel1, -jnp.inf, logits)
    v2 = jnp.max(rest, axis=0, keepdims=True)
    i2 = jnp.min(jnp.where(rest == v2, e_i, N_EXPERTS), axis=0, keepdims=True)
    sel2 = e_i == i2
    e2 = jnp.exp(v2 - v1)
    den = 1.0 + e2
    idx_ref[...] = jnp.concatenate([i1, i2], axis=0)
    gate_ref[...] = jnp.concatenate([1.0 / den, e2 / den], axis=0)

    sel = jnp.where(sel1 | sel2, 1.0, 0.0)
    t_r = lax.broadcasted_iota(jnp.int32, (tm, tm), 0)
    t_c = lax.broadcasted_iota(jnp.int32, (tm, tm), 1)
    before = jnp.where(t_r < t_c, 1.0, 0.0).astype(BF16)
    tot = base_ref[:, 0:1] + jnp.dot(sel.astype(BF16), before, preferred_element_type=F32)
    rk1 = jnp.sum(jnp.where(sel1, tot, 0.0), axis=0, keepdims=True)
    rk2 = jnp.sum(jnp.where(sel2, tot, 0.0), axis=0, keepdims=True)
    rank_ref[...] = jnp.concatenate([rk1, rk2], axis=0).astype(jnp.int32)
    new_base = base_ref[...] + jnp.sum(sel, axis=1, keepdims=True)
    base_ref[...] = new_base
    cnt_ref[...] = new_base.astype(jnp.int32)


def router(h, norm_g, router_w):
    n, d = h.shape
    tm = _tile(n, 512)
    row2 = lambda i: (0, i)
    idx, gate, rank, cnt = pl.pallas_call(
        functools.partial(_router_kernel, tm=tm),
        grid=(n // tm,),
        in_specs=[pl.BlockSpec((tm, d), lambda i: (i, 0)),
                  pl.BlockSpec((1, d), lambda i: (0, 0)),
                  pl.BlockSpec((N_EXPERTS, d), lambda i: (0, 0))],
        out_specs=[pl.BlockSpec((TOP_K, tm), row2), pl.BlockSpec((TOP_K, tm), row2),
                   pl.BlockSpec((TOP_K, tm), row2), pl.BlockSpec((N_EXPERTS, LANE), lambda i: (0, 0))],
        out_shape=[jax.ShapeDtypeStruct((TOP_K, n), jnp.int32), jax.ShapeDtypeStruct((TOP_K, n), F32),
                   jax.ShapeDtypeStruct((TOP_K, n), jnp.int32), jax.ShapeDtypeStruct((N_EXPERTS, LANE), jnp.int32)],
        scratch_shapes=[pltpu.VMEM((N_EXPERTS, LANE), F32)],
        compiler_params=_params(("arbitrary",)),
        name="router",
    )(h, norm_g.reshape(1, d), router_w.T)
    return idx, gate, rank, cnt[:, 0]


def _row_copy(src_ref, s, dst_ref, t, sem):
    return pltpu.make_async_copy(src_ref.at[pl.ds(s, 1), :], dst_ref.at[pl.ds(t, 1), :], sem)


def _dispatch_kernel(pos_ref, h_ref, xs_in_ref, xs_ref, sem, *, tt):
    del xs_in_ref

    def start(j, carry):
        for kk in range(TOP_K):
            _row_copy(h_ref, j, xs_ref, pos_ref[0, kk, j], sem).start()
        return carry

    def wait(j, carry):
        for kk in range(TOP_K):
            _row_copy(h_ref, j, xs_ref, pos_ref[0, kk, j], sem).wait()
        return carry

    lax.fori_loop(0, tt, start, 0, unroll=8)
    lax.fori_loop(0, tt, wait, 0, unroll=8)


def dispatch(h, pos_tiles, n_rows, tt):
    n, d = h.shape
    xs0 = jnp.zeros((n_rows, d), h.dtype)
    any_spec = pl.BlockSpec(memory_space=pl.ANY)
    return pl.pallas_call(
        functools.partial(_dispatch_kernel, tt=tt),
        grid=(n // tt,),
        in_specs=[pl.BlockSpec((1, TOP_K, tt), lambda i: (i, 0, 0), memory_space=pltpu.SMEM),
                  pl.BlockSpec((tt, d), lambda i: (i, 0)), any_spec],
        out_specs=any_spec,
        out_shape=jax.ShapeDtypeStruct((n_rows, d), h.dtype),
        scratch_shapes=[pltpu.SemaphoreType.DMA],
        input_output_aliases={2: 0},
        compiler_params=_params(("arbitrary",)),
        name="moe_dispatch",
    )(pos_tiles, h, xs0)


def _combine_kernel(pos_ref, h_ref, gate_ref, ys_ref, o_ref, buf_ref, sem, *, tt):
    def start(j, carry):
        for kk in range(TOP_K):
            _row_copy(ys_ref, pos_ref[0, kk, j], buf_ref.at[kk], j, sem).start()
        return carry

    def wait(j, carry):
        for kk in range(TOP_K):
            _row_copy(ys_ref, pos_ref[0, kk, j], buf_ref.at[kk], j, sem).wait()
        return carry

    lax.fori_loop(0, tt, start, 0)
    lax.fori_loop(0, tt, wait, 0)
    g = gate_ref[...]
    o_ref[...] = h_ref[...] + (g[:, 0:1] * buf_ref[0] + g[:, 1:2] * buf_ref[1])


def combine(h, gate_cols, ys, pos_tiles, tt):
    n, d = h.shape
    return pl.pallas_call(
        functools.partial(_combine_kernel, tt=tt),
        grid=(n // tt,),
        in_specs=[pl.BlockSpec((1, TOP_K, tt), lambda i: (i, 0, 0), memory_space=pltpu.SMEM),
                  pl.BlockSpec((tt, d), lambda i: (i, 0)),
                  pl.BlockSpec((tt, TOP_K), lambda i: (i, 0)),
                  pl.BlockSpec(memory_space=pl.ANY)],
        out_specs=pl.BlockSpec((tt, d), lambda i: (i, 0)),
        out_shape=jax.ShapeDtypeStruct((n, d), F32),
        scratch_shapes=[pltpu.VMEM((TOP_K, tt, d), F32), pltpu.SemaphoreType.DMA],
        compiler_params=_params(("arbitrary",)),
        name="moe_combine",
    )(pos_tiles, h, gate_cols, ys)


def moe_layer(h, norm_g, router_w, w_gate, w_up, w_down, tm=512, tf=512, tt=256):
    n, d = h.shape
    idx, gate, rank, counts = router(h, norm_g, router_w)
    tiles_e = (counts + tm - 1) // tm
    tile_end = jnp.cumsum(tiles_e)
    starts = (tile_end - tiles_e) * tm
    pos = rank
    for e in range(N_EXPERTS):
        pos = pos + jnp.where(idx == e, starts[e], 0)
    n_tiles = (TOP_K * n) // tm + N_EXPERTS
    tile_expert = jnp.minimum(jnp.searchsorted(tile_end, jnp.arange(n_tiles, dtype=jnp.int32), side="right"),
                              N_EXPERTS - 1).astype(jnp.int32)
    n_used = tile_end[-1:].astype(jnp.int32)
    tt = _tile(n, tt)
    pos_tiles = pos.reshape(TOP_K, n // tt, tt).transpose(1, 0, 2)
    xs = dispatch(h, pos_tiles, n_tiles * tm, tt)
    ys = grouped_ffn(xs, norm_g, w_gate, w_up, w_down, tile_expert, n_used, tm, tf)
    return combine(h, gate.T, ys, pos_tiles, tt)


def dense_ffn(h, norm_g, w_gate, w_up, w_down, tm=512, tf=512):
    n, d = h.shape
    tm = _tile(n, tm)
    n_tiles = n // tm
    return grouped_ffn(h, norm_g, w_gate[None], w_up[None], w_down[None],
                       jnp.zeros((n_tiles,), jnp.int32), jnp.full((1,), n_tiles, jnp.int32), tm, tf, resid=h)


def mlstm_layer(h, batch, tp, norm_g, w_in, b_i, b_f, h_gain, w_out):
    d = h.shape[1]
    n_main = 2 * ML_HEADS * ML_DQK + 2 * ML_HEADS * ML_DV
    hn = rmsnorm(h, norm_g, BF16)
    proj = matmul(hn, w_in.astype(BF16), n_main, BF16)
    gates = gate_proj(hn, w_in[:, n_main:].T.astype(BF16))
    gates = gates.reshape(2, ML_HEADS, batch, tp).transpose(0, 2, 1, 3).reshape(2, batch, ML_HEADS, 1, tp)
    y = mlstm_core(proj, gates[0], gates[1], b_i, b_f, h_gain, batch, tp)
    return matmul(y, w_out.astype(BF16), d, F32, resid=h)


def fox_layer(h, batch, tp, norm_g, w_in, b_f, q_gain, k_gain, w_out):
    d = h.shape[1]
    n_main = 4 * d
    hn = rmsnorm(h, norm_g, BF16)
    head_gain = jnp.concatenate([jnp.tile(q_gain * (FOX_DH ** -0.5 * LOG2E), FOX_HEADS), jnp.tile(k_gain, FOX_HEADS),
                                 jnp.ones((2 * d,), F32)]).reshape(1, n_main)
    proj = matmul(hn, w_in.astype(BF16), n_main, BF16, head_gain=head_gain, n_norm_cols=2 * d)
    fg = gate_proj(hn, w_in[:, n_main:].T.astype(BF16))
    fg = fg.reshape(FOX_HEADS, batch, tp).transpose(1, 0, 2)
    c = fox_cumsum(fg, b_f)
    y = fox_attention(proj, c, batch, tp)
    return matmul(y, w_out.astype(BF16), d, F32, resid=h)


def kernel(x, meta_tokens, ml_norm, ml_w_in, ml_b_i, ml_b_f, ml_h_gain, ml_w_out, ffn_norm, ffn_w_gate, ffn_w_up, ffn_w_down, fox_norm, fox_w_in, fox_b_f, fox_q_gain, fox_k_gain, fox_w_out, moe_norm, moe_router, moe_w_gate, moe_w_up, moe_w_down, final_norm):
    batch, seq, d = x.shape
    tp = FRONT + seq
    depth = ml_norm.shape[0] + fox_norm.shape[0]
    front = jnp.concatenate([jnp.zeros((N_DUMMY, d), x.dtype), meta_tokens.astype(x.dtype)], axis=0)
    h = jnp.concatenate([jnp.broadcast_to(front[None], (batch, FRONT, d)), x], axis=1).reshape(batch * tp, d)
    bf = lambda w: w.astype(BF16)
    for i in range(depth):
        j = i // 2
        if i % 2 == 0:
            h = mlstm_layer(h, batch, tp, ml_norm[j], ml_w_in[j], ml_b_i[j], ml_b_f[j], ml_h_gain[j], ml_w_out[j])
            h = dense_ffn(h, ffn_norm[j], bf(ffn_w_gate[j]), bf(ffn_w_up[j]), bf(ffn_w_down[j]))
        else:
            h = fox_layer(h, batch, tp, fox_norm[j], fox_w_in[j], fox_b_f[j], fox_q_gain[j], fox_k_gain[j], fox_w_out[j])
            h = moe_layer(h, moe_norm[j], moe_router[j], bf(moe_w_gate[j]), bf(moe_w_up[j]), bf(moe_w_down[j]))
    return final_rmsnorm(h.reshape(batch, tp, d), final_norm)
```

```python
import functools

import jax
import jax.numpy as jnp
from jax import lax
from jax.experimental import pallas as pl
from jax.experimental.pallas import tpu as pltpu

F32 = jnp.float32
BF16 = jnp.bfloat16

N_META = 16
EPS = 1e-6
LANE = 128
FRONT = 128
N_DUMMY = FRONT - N_META
ML_HEADS = 8
ML_DQK = 128
ML_DV = 256
ML_CHUNK = 128
GATE_CAP = 15.0
NEG_BIG = -1e30
FOX_HEADS = 32
FOX_DH = 64
FOX_PAIRS = FOX_HEADS // 2
FOX_ROWS = 16
LOG2E = 1.4426950408889634
N_EXPERTS = 8
TOP_K = 2

VMEM_LIMIT = 56 * 1024 * 1024


def _params(sem, vmem=VMEM_LIMIT):
    return pltpu.CompilerParams(dimension_semantics=sem, vmem_limit_bytes=vmem)


def _tile(n, pref):
    best = None
    t = LANE
    while t <= min(n, pref):
        if n % t == 0:
            best = t
        t += LANE
    assert best is not None, (n, pref)
    return best


def _split3(x):
    x1 = x.astype(BF16)
    r1 = x - x1.astype(F32)
    x2 = r1.astype(BF16)
    r2 = r1 - x2.astype(F32)
    return x1, x2, r2.astype(BF16)


def _log_sigmoid(x):
    return -(jnp.maximum(-x, 0.0) + jnp.log1p(jnp.exp(-jnp.abs(x))))


def _sigmoid(x):
    return 1.0 / (1.0 + jnp.exp(-x))


def _rmsnorm_kernel(x_ref, g_ref, o_ref):
    x = x_ref[...]
    ms = jnp.mean(x * x, axis=-1, keepdims=True)
    o_ref[...] = (x * lax.rsqrt(ms + EPS) * g_ref[...]).astype(o_ref.dtype)


def rmsnorm(x, g, out_dtype):
    n, d = x.shape
    tm = _tile(n, 512)
    return pl.pallas_call(
        _rmsnorm_kernel,
        grid=(n // tm,),
        in_specs=[pl.BlockSpec((tm, d), lambda i: (i, 0)),
                  pl.BlockSpec((1, d), lambda i: (0, 0))],
        out_specs=pl.BlockSpec((tm, d), lambda i: (i, 0)),
        out_shape=jax.ShapeDtypeStruct((n, d), out_dtype),
        compiler_params=_params(("parallel",)),
        name="rmsnorm",
    )(x, g.reshape(1, d))


def _final_norm_kernel(x_ref, g_ref, o_ref):
    x = x_ref[0]
    ms = jnp.mean(x * x, axis=-1, keepdims=True)
    o_ref[0] = x * lax.rsqrt(ms + EPS) * g_ref[...]


def final_rmsnorm(h3, g):
    b, tp, d = h3.shape
    nblk = (tp - FRONT) // LANE
    return pl.pallas_call(
        _final_norm_kernel,
        grid=(b, nblk),
        in_specs=[pl.BlockSpec((1, LANE, d), lambda i, j: (i, j + FRONT // LANE, 0)),
                  pl.BlockSpec((1, d), lambda i, j: (0, 0))],
        out_specs=pl.BlockSpec((1, LANE, d), lambda i, j: (i, j, 0)),
        out_shape=jax.ShapeDtypeStruct((b, tp - FRONT, d), F32),
        compiler_params=_params(("parallel", "parallel")),
        name="final_norm",
    )(h3, g.reshape(1, d))


def _group_sum_matrix():
    r = lax.broadcasted_iota(jnp.int32, (LANE, LANE), 0) // FOX_DH
    c = lax.broadcasted_iota(jnp.int32, (LANE, LANE), 1) // FOX_DH
    return jnp.where(r == c, 1.0, 0.0).astype(BF16)


def _matmul_kernel(*refs, has_resid, n_norm_tiles):
    x_ref, w_ref = refs[0], refs[1]
    k = 2
    gain_ref = resid_ref = None
    if n_norm_tiles:
        gain_ref = refs[k]; k += 1
    if has_resid:
        resid_ref = refs[k]; k += 1
    o_ref = refs[k]
    acc = jnp.dot(x_ref[...], w_ref[...], preferred_element_type=F32)
    if has_resid:
        acc = resid_ref[...] + acc

    if not n_norm_tiles:
        o_ref[...] = acc.astype(o_ref.dtype)
        return

    j = pl.program_id(0)

    @pl.when(j >= n_norm_tiles)
    def _():
        o_ref[...] = acc.astype(o_ref.dtype)

    @pl.when(j < n_norm_tiles)
    def _():
        gmat = _group_sum_matrix()
        tn = acc.shape[1]
        for c in range(tn // LANE):
            y = acc[:, c * LANE:(c + 1) * LANE]
            y2 = y * y
            hi = y2.astype(BF16)
            lo = (y2 - hi.astype(F32)).astype(BF16)
            ss = (jnp.dot(hi, gmat, preferred_element_type=F32)
                  + jnp.dot(lo, gmat, preferred_element_type=F32))
            yn = y * lax.rsqrt(ss * (1.0 / FOX_DH) + EPS) * gain_ref[:, c * LANE:(c + 1) * LANE]
            o_ref[:, c * LANE:(c + 1) * LANE] = yn.astype(o_ref.dtype)


def matmul(x, w, n_cols, out_dtype, resid=None, head_gain=None, n_norm_cols=0, tm_pref=512, tn_pref=1024):
    m, kdim = x.shape
    tm = _tile(m, tm_pref)
    tn = _tile(n_cols, tn_pref)
    n_norm_tiles = n_norm_cols // tn
    assert n_norm_tiles * tn == n_norm_cols
    in_specs = [pl.BlockSpec((tm, kdim), lambda j, i: (i, 0)),
                pl.BlockSpec((kdim, tn), lambda j, i: (0, j))]
    args = [x, w]
    if n_norm_tiles:
        in_specs.append(pl.BlockSpec((1, tn), lambda j, i: (0, j)))
        args.append(head_gain)
    if resid is not None:
        in_specs.append(pl.BlockSpec((tm, tn), lambda j, i: (i, j)))
        args.append(resid)
    return pl.pallas_call(
        functools.partial(_matmul_kernel, has_resid=resid is not None, n_norm_tiles=n_norm_tiles),
        grid=(n_cols // tn, m // tm),
        in_specs=in_specs,
        out_specs=pl.BlockSpec((tm, tn), lambda j, i: (i, j)),
        out_shape=jax.ShapeDtypeStruct((m, n_cols), out_dtype),
        compiler_params=_params(("parallel", "parallel")),
        name="matmul",
    )(*args)


def _gate_proj_kernel(w_ref, x_ref, o_ref):
    o_ref[...] = lax.dot_general(w_ref[...], x_ref[...], (((1,), (1,)), ((), ())),
                                 preferred_element_type=F32)


def gate_proj(x, w_t):
    m, kdim = x.shape
    g = w_t.shape[0]
    tm = _tile(m, 512)
    return pl.pallas_call(
        _gate_proj_kernel,
        grid=(m // tm,),
        in_specs=[pl.BlockSpec((g, kdim), lambda i: (0, 0)),
                  pl.BlockSpec((tm, kdim), lambda i: (i, 0))],
        out_specs=pl.BlockSpec((g, tm), lambda i: (0, i)),
        out_shape=jax.ShapeDtypeStruct((g, m), F32),
        compiler_params=_params(("parallel",)),
        name="gate_proj",
    )(w_t, x)


def _mlstm_kernel(bi_ref, bf_ref, q_ref, k_ref, v_ref, og_ref, ig_ref, fg_ref, gain_ref,
                  out_ref, c_ref, m_ref):
    L = ML_CHUNK
    hd = pl.program_id(1)
    ci = pl.program_id(2)

    @pl.when(ci == 0)
    def _():
        c_ref[...] = jnp.zeros_like(c_ref)
        m_ref[...] = jnp.zeros_like(m_ref)

    pos = ci * L + lax.broadcasted_iota(jnp.int32, (1, L), 1)
    dummy = pos < N_DUMMY
    ig = ig_ref[0, 0] + bi_ref[hd]
    fg = fg_ref[0, 0] + bf_ref[hd]
    li = jnp.where(dummy, NEG_BIG, GATE_CAP * jnp.tanh(ig / GATE_CAP))
    lf = jnp.where(dummy, 0.0, _log_sigmoid(GATE_CAP * jnp.tanh(fg / GATE_CAP)))

    t_i = lax.broadcasted_iota(jnp.int32, (L, L), 0)
    s_i = lax.broadcasted_iota(jnp.int32, (L, L), 1)
    tril = s_i <= t_i
    b_col = jnp.sum(jnp.where(tril, jnp.broadcast_to(lf, (L, L)), 0.0), axis=1, keepdims=True)
    b_t = jnp.broadcast_to(b_col, (L, L))
    b_s = b_t.T
    d_log = jnp.where(tril, b_t - b_s + li, -jnp.inf)
    m_prev = m_ref[0:1, 0:1]
    inter = b_col + m_prev
    m_t = jnp.maximum(inter, jnp.max(d_log, axis=1, keepdims=True))

    q = q_ref[0]
    k = k_ref[0]
    scale = ML_DQK ** -0.5
    s = lax.dot_general(q, k, (((1,), (1,)), ((), ())), preferred_element_type=F32) * scale
    p = (s * jnp.exp(d_log - m_t)).astype(BF16)
    w_inter = jnp.exp(inter - m_t)

    lane = lax.broadcasted_iota(jnp.int32, (L, LANE), 1)
    v_ext = jnp.concatenate([v_ref[0], jnp.where(lane == 0, 1.0, 0.0).astype(BF16)], axis=1)
    c_prev = c_ref[...]
    num_ext = (jnp.dot(p, v_ext, preferred_element_type=F32)
               + w_inter * jnp.dot(q, c_prev.astype(BF16), preferred_element_type=F32))
    num = num_ext[:, :ML_DV]
    den = num_ext[:, ML_DV:ML_DV + 1]
    hh = num / jnp.maximum(jnp.abs(den), jnp.exp(-m_t))
    ms = jnp.mean(hh * hh, axis=1, keepdims=True)
    hn = hh * lax.rsqrt(ms + EPS) * gain_ref[...]
    out_ref[0] = (_sigmoid(og_ref[0].astype(F32)) * hn).astype(out_ref.dtype)

    g_tot = jnp.sum(lf, axis=1, keepdims=True)
    a = g_tot - b_s[0:1, :] + li
    m_loc = jnp.max(a, axis=1, keepdims=True)
    w_loc = jnp.exp(a - m_loc)
    k_t = (k.astype(F32) * scale).T
    c_loc = jnp.dot((k_t * w_loc).astype(BF16), v_ext, preferred_element_type=F32)
    m_new = jnp.maximum(g_tot + m_prev, m_loc)
    sp = jnp.exp(g_tot + m_prev - m_new)
    sl = jnp.exp(m_loc - m_new)
    c_ref[...] = sp * c_prev + sl * c_loc
    m_ref[...] = jnp.broadcast_to(m_new, m_ref.shape)


def mlstm_core(proj, ig, fg, b_i, b_f, h_gain, batch, tp):
    L = ML_CHUNK
    proj3 = proj.reshape(batch, tp, proj.shape[1])
    nqk = ML_HEADS
    nv = (2 * ML_HEADS * ML_DQK) // ML_DV
    smem = pl.BlockSpec(memory_space=pltpu.SMEM)
    out = pl.pallas_call(
        _mlstm_kernel,
        grid=(batch, ML_HEADS, tp // L),
        in_specs=[smem, smem,
                  pl.BlockSpec((1, L, ML_DQK), lambda b, h, c: (b, c, h)),
                  pl.BlockSpec((1, L, ML_DQK), lambda b, h, c: (b, c, nqk + h)),
                  pl.BlockSpec((1, L, ML_DV), lambda b, h, c: (b, c, nv + h)),
                  pl.BlockSpec((1, L, ML_DV), lambda b, h, c: (b, c, nv + ML_HEADS + h)),
                  pl.BlockSpec((1, 1, 1, L), lambda b, h, c: (b, h, 0, c)),
                  pl.BlockSpec((1, 1, 1, L), lambda b, h, c: (b, h, 0, c)),
                  pl.BlockSpec((1, ML_DV), lambda b, h, c: (0, h))],
        out_specs=pl.BlockSpec((1, L, ML_DV), lambda b, h, c: (b, c, h)),
        out_shape=jax.ShapeDtypeStruct((batch, tp, ML_HEADS * ML_DV), BF16),
        scratch_shapes=[pltpu.VMEM((ML_DQK, ML_DV + LANE), F32), pltpu.VMEM((8, LANE), F32)],
        compiler_params=_params(("parallel", "parallel", "arbitrary")),
        name="mlstm_core",
    )(b_i, b_f, proj3, proj3, proj3, proj3, ig, fg, h_gain.reshape(1, -1))
    return out.reshape(batch * tp, ML_HEADS * ML_DV)


def _fox_cumsum_kernel(fg_ref, bf_ref, o_ref, carry_ref):
    ci = pl.program_id(1)

    @pl.when(ci == 0)
    def _():
        carry_ref[...] = jnp.zeros_like(carry_ref)

    pos = ci * LANE + lax.broadcasted_iota(jnp.int32, (LANE, 1), 0)
    x = jnp.where(pos < N_DUMMY, 0.0, LOG2E * _log_sigmoid(fg_ref[0] + bf_ref[...]))
    r = lax.broadcasted_iota(jnp.int32, (LANE, LANE), 0)
    c = lax.broadcasted_iota(jnp.int32, (LANE, LANE), 1)
    lower = jnp.where(c <= r, 1.0, 0.0).astype(BF16)
    x1, x2, x3 = _split3(x)
    cs = (jnp.dot(lower, x1, preferred_element_type=F32)
          + jnp.dot(lower, x2, preferred_element_type=F32)
          + jnp.dot(lower, x3, preferred_element_type=F32)) + carry_ref[0:1, :]
    carry_ref[...] = jnp.broadcast_to(cs[LANE - 1:LANE, :], carry_ref.shape)
    for h in range(FOX_HEADS):
        o_ref[0, h] = jnp.broadcast_to(cs[:, h:h + 1], (LANE, LANE))


def fox_cumsum(fg, b_f, batch, tp):
    fg3 = fg.reshape(batch, tp, LANE)
    bias = jnp.zeros((1, LANE), F32).at[0, :FOX_HEADS].set(b_f)
    return pl.pallas_call(
        _fox_cumsum_kernel,
        grid=(batch, tp // LANE),
        in_specs=[pl.BlockSpec((1, LANE, LANE), lambda i, c: (i, c, 0)),
                  pl.BlockSpec((1, LANE), lambda i, c: (0, 0))],
        out_specs=pl.BlockSpec((1, FOX_HEADS, LANE, LANE), lambda i, c: (i, 0, c, 0)),
        out_shape=jax.ShapeDtypeStruct((batch, FOX_HEADS, tp, LANE), F32),
        scratch_shapes=[pltpu.VMEM((8, LANE), F32)],
        compiler_params=_params(("parallel", "arbitrary")),
        name="fox_cumsum",
    )(fg3, bias)


def _fox_attn_kernel(q_ref, k_ref, v_ref, og_ref, cb_ref, o_ref,
                     qm_ref, vt_ref, ka_ref, s_ref, m_ref, l_ref, acc_ref, *, blk, nblk):
    qi = pl.program_id(2)
    nt = (((1,), (1,)), ((), ()))
    sub = blk // LANE
    lane = lax.broadcasted_iota(jnp.int32, (1, LANE), 1)
    lo = lane < FOX_DH
    bias_lane = (FOX_DH, 0)

    @pl.when(qi == 0)
    def _():
        row = lax.broadcasted_iota(jnp.int32, (LANE, 1), 0)
        for j in range(nblk * sub):
            rows = slice(j * LANE, (j + 1) * LANE)
            dst = (j // sub, slice((j % sub) * LANE, (j % sub + 1) * LANE))
            vt_ref[dst[0], :, dst[1]] = v_ref[0, rows, :].astype(F32).T.astype(BF16)
            kt = k_ref[0, rows, :].astype(F32)
            for hh in range(2):
                cb = cb_ref[0, hh, rows, :]
                if j == 0:
                    cb = jnp.where(row < N_DUMMY, -NEG_BIG, cb)
                c1, c2, c3 = (t.astype(F32) for t in _split3(cb))
                b0 = bias_lane[hh]
                aug = jnp.where(lane == b0, c1, jnp.where(lane == b0 + 1, c2,
                                jnp.where(lane == b0 + 2, c3, 0.0)))
                ka = jnp.where(lo if hh == 0 else jnp.logical_not(lo), kt, aug)
                ka_ref[hh, dst[0], dst[1], :] = ka.astype(BF16)

    q = q_ref[0].astype(F32)
    qm_ref[0] = jnp.where(lo, q, jnp.where((lane >= bias_lane[0]) & (lane < bias_lane[0] + 3), -1.0, 0.0)).astype(BF16)
    qm_ref[1] = jnp.where(lo, jnp.where(lane < bias_lane[1] + 3, -1.0, 0.0), q).astype(BF16)
    m_ref[...] = jnp.full_like(m_ref, -jnp.inf)
    l_ref[...] = jnp.zeros_like(l_ref)
    acc_ref[...] = jnp.zeros_like(acc_ref)

    def scores(ki, slot):
        for hh in range(2):
            s_ref[slot, hh] = lax.dot_general(ka_ref[hh, ki], qm_ref[hh], nt, preferred_element_type=F32)

    def softmax_pv(ki, slot, diagonal):
        vt = vt_ref[ki]
        if diagonal:
            spos = lax.broadcasted_iota(jnp.int32, (blk, 1), 0)
            tpos = lax.broadcasted_iota(jnp.int32, (1, blk), 1)
            off = ki * blk
            valid = (spos <= tpos) & ((spos + off >= N_DUMMY) | (tpos + off < N_DUMMY))
        for hh in range(2):
            s = s_ref[slot, hh]
            if diagonal:
                s = jnp.where(valid, s, -jnp.inf)
            m_prev = m_ref[hh]
            m_new = jnp.maximum(m_prev, jnp.max(s, axis=0, keepdims=True))
            alpha = jnp.exp2(m_prev - m_new)
            p = jnp.exp2(s - m_new)
            l_ref[hh] = alpha * l_ref[hh] + jnp.sum(p, axis=0, keepdims=True)
            m_ref[hh] = m_new
            pv = jnp.dot(vt[hh * FOX_DH:(hh + 1) * FOX_DH, :], p.astype(BF16), preferred_element_type=F32)
            acc_ref[hh] = acc_ref[hh] * alpha + pv

    scores(0, 0)

    def body(i, carry):
        scores(2 * i + 1, 1)
        softmax_pv(2 * i, 0, False)
        scores(2 * i + 2, 0)
        softmax_pv(2 * i + 1, 1, False)
        return carry

    lax.fori_loop(0, qi // 2, body, 0)

    @pl.when(qi % 2 == 0)
    def _():
        softmax_pv(qi, 0, True)

    @pl.when(qi % 2 == 1)
    def _():
        scores(qi, 1)
        softmax_pv(qi - 1, 0, False)
        softmax_pv(qi, 1, True)

    y_t = jnp.concatenate([acc_ref[0] / l_ref[0], acc_ref[1] / l_ref[1]], axis=0)
    y = jnp.concatenate([y_t[:, j * LANE:(j + 1) * LANE].T for j in range(sub)], axis=0)
    o_ref[0] = (_sigmoid(og_ref[0].astype(F32)) * y).astype(o_ref.dtype)


def fox_attention(proj, cb, batch, tp):
    d = FOX_HEADS * FOX_DH
    blk = _tile(tp, 512)
    nblk = tp // blk
    proj3 = proj.reshape(batch, tp, proj.shape[1])
    out = pl.pallas_call(
        functools.partial(_fox_attn_kernel, blk=blk, nblk=nblk),
        grid=(batch, FOX_PAIRS, nblk),
        in_specs=[pl.BlockSpec((1, blk, LANE), lambda b, p, qi: (b, qi, p)),
                  pl.BlockSpec((1, tp, LANE), lambda b, p, qi: (b, 0, FOX_PAIRS + p)),
                  pl.BlockSpec((1, tp, LANE), lambda b, p, qi: (b, 0, 2 * FOX_PAIRS + p)),
                  pl.BlockSpec((1, blk, LANE), lambda b, p, qi: (b, qi, 3 * FOX_PAIRS + p)),
                  pl.BlockSpec((1, 2, tp, LANE), lambda b, p, qi: (b, p, 0, 0))],
        out_specs=pl.BlockSpec((1, blk, LANE), lambda b, p, qi: (b, qi, p)),
        out_shape=jax.ShapeDtypeStruct((batch, tp, d), BF16),
        scratch_shapes=[pltpu.VMEM((2, blk, LANE), BF16),
                        pltpu.VMEM((nblk, LANE, blk), BF16),
                        pltpu.VMEM((2, nblk, blk, LANE), BF16),
                        pltpu.VMEM((2, 2, blk, blk), F32),
                        pltpu.VMEM((2, 1, blk), F32), pltpu.VMEM((2, 1, blk), F32),
                        pltpu.VMEM((2, FOX_DH, blk), F32)],
        compiler_params=_params(("arbitrary", "arbitrary", "arbitrary")),
        name="fox_attention",
    )(proj3, proj3, proj3, proj3, cb)
    return out.reshape(batch * tp, d)


def _ffn_kernel(te_ref, nu_ref, x_ref, g_ref, wg_ref, wu_ref, wd_ref, *rest, has_resid):
    if has_resid:
        resid_ref, o_ref, xb_ref = rest
    else:
        (o_ref, xb_ref), resid_ref = rest, None
    i = pl.program_id(0)
    f = pl.program_id(1)

    @pl.when(i < nu_ref[0])
    def _():
        @pl.when(f == 0)
        def _():
            x = x_ref[...]
            ms = jnp.mean(x * x, axis=-1, keepdims=True)
            xb_ref[...] = (x * lax.rsqrt(ms + EPS) * g_ref[...]).astype(BF16)

        xb = xb_ref[...]
        gate = jnp.dot(xb, wg_ref[0].astype(BF16), preferred_element_type=F32)
        up = jnp.dot(xb, wu_ref[0].astype(BF16), preferred_element_type=F32)
        act = (gate * _sigmoid(gate) * up).astype(BF16)
        contrib = jnp.dot(act, wd_ref[0].astype(BF16), preferred_element_type=F32)

        @pl.when(f == 0)
        def _():
            if has_resid:
                o_ref[...] = resid_ref[...] + contrib
            else:
                o_ref[...] = contrib

        @pl.when(f > 0)
        def _():
            o_ref[...] += contrib

    @pl.when((i >= nu_ref[0]) & (f == 0))
    def _():
        o_ref[...] = jnp.zeros_like(o_ref)


def grouped_ffn(x, norm_g, w_gate, w_up, w_down, tile_expert, n_used, tm, tf, resid=None):
    r, d = x.shape
    ff = w_gate.shape[2]
    n_tiles = r // tm
    n_ff = ff // tf
    assert n_tiles * tm == r and n_ff * tf == ff

    def row_map(i, f, te, nu):
        return (jnp.minimum(i, nu[0] - 1), 0)

    def _ef(i, f, te, nu):
        ii = jnp.minimum(i, nu[0] - 1)
        return te[ii], jnp.where(i < nu[0], f, n_ff - 1)

    def wcol_map(i, f, te, nu):
        e, fi = _ef(i, f, te, nu)
        return (e, 0, fi)

    def wrow_map(i, f, te, nu):
        e, fi = _ef(i, f, te, nu)
        return (e, fi, 0)

    in_specs = [pl.BlockSpec((tm, d), row_map),
                pl.BlockSpec((1, d), lambda i, f, te, nu: (0, 0)),
                pl.BlockSpec((1, d, tf), wcol_map),
                pl.BlockSpec((1, d, tf), wcol_map),
                pl.BlockSpec((1, tf, d), wrow_map)]
    args = [x, norm_g.reshape(1, d), w_gate, w_up, w_down]
    if resid is not None:
        in_specs.append(pl.BlockSpec((tm, d), row_map))
        args.append(resid)
    return pl.pallas_call(
        functools.partial(_ffn_kernel, has_resid=resid is not None),
        grid_spec=pltpu.PrefetchScalarGridSpec(
            num_scalar_prefetch=2,
            grid=(n_tiles, n_ff),
            in_specs=in_specs,
            out_specs=pl.BlockSpec((tm, d), lambda i, f, te, nu: (i, 0)),
            scratch_shapes=[pltpu.VMEM((tm, d), BF16)]),
        out_shape=jax.ShapeDtypeStruct((r, d), F32),
        compiler_params=_params(("arbitrary", "arbitrary")),
        name="grouped_ffn",
    )(tile_expert, n_used, *args)


def _router_kernel(x_ref, g_ref, rt_ref, idx_ref, gate_ref, rank_ref, cnt_ref, base_ref, *, tm):
    i = pl.program_id(0)

    @pl.when(i == 0)
    def _():
        base_ref[...] = jnp.zeros_like(base_ref)

    x = x_ref[...]
    ms = jnp.mean(x * x, axis=-1, keepdims=True)
    hn = x * lax.rsqrt(ms + EPS) * g_ref[...]
    r1, r2, _ = _split3(rt_ref[...])
    h1, h2, _ = _split3(hn)
    nt = (((1,), (1,)), ((), ()))
    logits = (lax.dot_general(r1, h1, nt, preferred_element_type=F32)
              + lax.dot_general(r1, h2, nt, preferred_element_type=F32)
              + lax.dot_general(r2, h1, nt, preferred_element_type=F32))

    e_i = lax.broadcasted_iota(jnp.int32, (N_EXPERTS, tm), 0)
    v1 = jnp.max(logits, axis=0, keepdims=True)
    i1 = jnp.min(jnp.where(logits == v1, e_i, N_EXPERTS), axis=0, keepdims=True)
    sel1 = e_i == i1
    rest = jnp.where(sel1, -jnp.inf, logits)
    v2 = jnp.max(rest, axis=0, keepdims=True)
    i2 = jnp.min(jnp.where(rest == v2, e_i, N_EXPERTS), axis=0, keepdims=True)
    sel2 = e_i == i2
    e2 = jnp.exp(v2 - v1)
    den = 1.0 + e2
    idx_ref[...] = jnp.concatenate([i1, i2], axis=0)
    gate_ref[...] = jnp.concatenate([1.0 / den, e2 / den], axis=0)

    sel = jnp.where(sel1 | sel2, 1.0, 0.0)
    t_r = lax.broadcasted_iota(jnp.int32, (tm, tm), 0)
    t_c = lax.broadcasted_iota(jnp.int32, (tm, tm), 1)
    before = jnp.where(t_r < t_c, 1.0, 0.0).astype(BF16)
    tot = base_ref[:, 0:1] + jnp.dot(sel.astype(BF16), before, preferred_element_type=F32)
    rk1 = jnp.sum(jnp.where(sel1, tot, 0.0), axis=0, keepdims=True)
    rk2 = jnp.sum(jnp.where(sel2, tot, 0.0), axis=0, keepdims=True)
    rank_ref[...] = jnp.concatenate([rk1, rk2], axis=0).astype(jnp.int32)
    new_base = base_ref[...] + jnp.sum(sel, axis=1, keepdims=True)
    base_ref[...] = new_base
    cnt_ref[...] = new_base.astype(jnp.int32)


def router(h, norm_g, router_w):
    n, d = h.shape
    tm = _tile(n, 512)
    row2 = lambda i: (0, i)
    idx, gate, rank, cnt = pl.pallas_call(
        functools.partial(_router_kernel, tm=tm),
        grid=(n // tm,),
        in_specs=[pl.BlockSpec((tm, d), lambda i: (i, 0)),
                  pl.BlockSpec((1, d), lambda i: (0, 0)),
                  pl.BlockSpec((N_EXPERTS, d), lambda i: (0, 0))],
        out_specs=[pl.BlockSpec((TOP_K, tm), row2), pl.BlockSpec((TOP_K, tm), row2),
                   pl.BlockSpec((TOP_K, tm), row2), pl.BlockSpec((N_EXPERTS, LANE), lambda i: (0, 0))],
        out_shape=[jax.ShapeDtypeStruct((TOP_K, n), jnp.int32), jax.ShapeDtypeStruct((TOP_K, n), F32),
                   jax.ShapeDtypeStruct((TOP_K, n), jnp.int32), jax.ShapeDtypeStruct((N_EXPERTS, LANE), jnp.int32)],
        scratch_shapes=[pltpu.VMEM((N_EXPERTS, LANE), F32)],
        compiler_params=_params(("arbitrary",)),
        name="router",
    )(h, norm_g.reshape(1, d), router_w.T)
    return idx, gate, rank, cnt[:, 0]


def _row_copy(src_ref, s, dst_ref, t, sem):
    return pltpu.make_async_copy(src_ref.at[pl.ds(s, 1), :], dst_ref.at[pl.ds(t, 1), :], sem)


def _dispatch_kernel(pos_ref, h_ref, xs_in_ref, xs_ref, sem, *, tt):
    del xs_in_ref

    def start(j, carry):
        for kk in range(TOP_K):
            _row_copy(h_ref, j, xs_ref, pos_ref[0, kk, j], sem).start()
        return carry

    def wait(j, carry):
        for kk in range(TOP_K):
            _row_copy(h_ref, j, xs_ref, pos_ref[0, kk, j], sem).wait()
        return carry

    lax.fori_loop(0, tt, start, 0, unroll=8)
    lax.fori_loop(0, tt, wait, 0, unroll=8)


def dispatch(h, pos_tiles, n_rows, tt):
    n, d = h.shape
    xs0 = jnp.zeros((n_rows, d), h.dtype)
    any_spec = pl.BlockSpec(memory_space=pl.ANY)
    return pl.pallas_call(
        functools.partial(_dispatch_kernel, tt=tt),
        grid=(n // tt,),
        in_specs=[pl.BlockSpec((1, TOP_K, tt), lambda i: (i, 0, 0), memory_space=pltpu.SMEM),
                  pl.BlockSpec((tt, d), lambda i: (i, 0)), any_spec],
        out_specs=any_spec,
        out_shape=jax.ShapeDtypeStruct((n_rows, d), h.dtype),
        scratch_shapes=[pltpu.SemaphoreType.DMA],
        input_output_aliases={2: 0},
        compiler_params=_params(("arbitrary",)),
        name="moe_dispatch",
    )(pos_tiles, h, xs0)


def _combine_kernel(pos_ref, h_ref, gate_ref, ys_ref, o_ref, buf_ref, sem, *, tt):
    def start(j, carry):
        for kk in range(TOP_K):
            _row_copy(ys_ref, pos_ref[0, kk, j], buf_ref.at[kk], j, sem).start()
        return carry

    def wait(j, carry):
        for kk in range(TOP_K):
            _row_copy(ys_ref, pos_ref[0, kk, j], buf_ref.at[kk], j, sem).wait()
        return carry

    lax.fori_loop(0, tt, start, 0)
    lax.fori_loop(0, tt, wait, 0)
    g = gate_ref[...]
    o_ref[...] = h_ref[...] + (g[:, 0:1] * buf_ref[0] + g[:, 1:2] * buf_ref[1])


def combine(h, gate_cols, ys, pos_tiles, tt):
    n, d = h.shape
    return pl.pallas_call(
        functools.partial(_combine_kernel, tt=tt),
        grid=(n // tt,),
        in_specs=[pl.BlockSpec((1, TOP_K, tt), lambda i: (i, 0, 0), memory_space=pltpu.SMEM),
                  pl.BlockSpec((tt, d), lambda i: (i, 0)),
                  pl.BlockSpec((tt, TOP_K), lambda i: (i, 0)),
                  pl.BlockSpec(memory_space=pl.ANY)],
        out_specs=pl.BlockSpec((tt, d), lambda i: (i, 0)),
        out_shape=jax.ShapeDtypeStruct((n, d), F32),
        scratch_shapes=[pltpu.VMEM((TOP_K, tt, d), F32), pltpu.SemaphoreType.DMA],
        compiler_params=_params(("arbitrary",)),
        name="moe_combine",
    )(pos_tiles, h, gate_cols, ys)


def moe_layer(h, norm_g, router_w, w_gate, w_up, w_down, tm=512, tf=512, tt=256):
    n, d = h.shape
    idx, gate, rank, counts = router(h, norm_g, router_w)
    tiles_e = (counts + tm - 1) // tm
    tile_end = jnp.cumsum(tiles_e)
    starts = (tile_end - tiles_e) * tm
    pos = rank
    for e in range(N_EXPERTS):
        pos = pos + jnp.where(idx == e, starts[e], 0)
    n_tiles = (TOP_K * n) // tm + N_EXPERTS
    tile_expert = jnp.minimum(jnp.searchsorted(tile_end, jnp.arange(n_tiles, dtype=jnp.int32), side="right"),
                              N_EXPERTS - 1).astype(jnp.int32)
    n_used = tile_end[-1:].astype(jnp.int32)
    tt = _tile(n, tt)
    pos_tiles = pos.reshape(TOP_K, n // tt, tt).transpose(1, 0, 2)
    xs = dispatch(h, pos_tiles, n_tiles * tm, tt)
    ys = grouped_ffn(xs, norm_g, w_gate, w_up, w_down, tile_expert, n_used, tm, tf)
    return combine(h, gate.T, ys, pos_tiles, tt)


def dense_ffn(h, norm_g, w_gate, w_up, w_down, tm=512, tf=512):
    n, d = h.shape
    tm = _tile(n, tm)
    n_tiles = n // tm
    return grouped_ffn(h, norm_g, w_gate[None], w_up[None], w_down[None],
                       jnp.zeros((n_tiles,), jnp.int32), jnp.full((1,), n_tiles, jnp.int32), tm, tf, resid=h)


def mlstm_layer(h, batch, tp, norm_g, w_in, b_i, b_f, h_gain, w_out):
    d = h.shape[1]
    n_main = 2 * ML_HEADS * ML_DQK + 2 * ML_HEADS * ML_DV
    hn = rmsnorm(h, norm_g, BF16)
    proj = matmul(hn, w_in.astype(BF16), n_main, BF16)
    gates = gate_proj(hn, w_in[:, n_main:].T.astype(BF16))
    gates = gates.reshape(2, ML_HEADS, batch, tp).transpose(0, 2, 1, 3).reshape(2, batch, ML_HEADS, 1, tp)
    y = mlstm_core(proj, gates[0], gates[1], b_i, b_f, h_gain, batch, tp)
    return matmul(y, w_out.astype(BF16), d, F32, resid=h)


def fox_layer(h, batch, tp, norm_g, w_in, b_f, q_gain, k_gain, w_out):
    d = h.shape[1]
    n_main = 4 * d
    hn = rmsnorm(h, norm_g, BF16)
    head_gain = jnp.concatenate([jnp.tile(q_gain * (FOX_DH ** -0.5 * LOG2E), FOX_HEADS), jnp.tile(k_gain, FOX_HEADS),
                                 jnp.ones((2 * d,), F32)]).reshape(1, n_main)
    proj = matmul(hn, w_in.astype(BF16), n_main, BF16, head_gain=head_gain, n_norm_cols=2 * d)
    w_fg = jnp.zeros((d, LANE), BF16).at[:, :FOX_HEADS].set(w_in[:, n_main:].astype(BF16))
    fg = matmul(hn, w_fg, LANE, F32)
    cb = fox_cumsum(fg, b_f, batch, tp)
    y = fox_attention(proj, cb, batch, tp)
    return matmul(y, w_out.astype(BF16), d, F32, resid=h)


def kernel(x, meta_tokens, ml_norm, ml_w_in, ml_b_i, ml_b_f, ml_h_gain, ml_w_out, ffn_norm, ffn_w_gate, ffn_w_up, ffn_w_down, fox_norm, fox_w_in, fox_b_f, fox_q_gain, fox_k_gain, fox_w_out, moe_norm, moe_router, moe_w_gate, moe_w_up, moe_w_down, final_norm):
    batch, seq, d = x.shape
    tp = FRONT + seq
    depth = ml_norm.shape[0] + fox_norm.shape[0]
    front = jnp.concatenate([jnp.zeros((N_DUMMY, d), x.dtype), meta_tokens.astype(x.dtype)], axis=0)
    h = jnp.concatenate([jnp.broadcast_to(front[None], (batch, FRONT, d)), x], axis=1).reshape(batch * tp, d)
    bf = lambda w: w.astype(BF16)
    for i in range(depth):
        j = i // 2
        if i % 2 == 0:
            h = mlstm_layer(h, batch, tp, ml_norm[j], ml_w_in[j], ml_b_i[j], ml_b_f[j], ml_h_gain[j], ml_w_out[j])
            h = dense_ffn(h, ffn_norm[j], bf(ffn_w_gate[j]), bf(ffn_w_up[j]), bf(ffn_w_down[j]))
        else:
            h = fox_layer(h, batch, tp, fox_norm[j], fox_w_in[j], fox_b_f[j], fox_q_gain[j], fox_k_gain[j], fox_w_out[j])
            h = moe_layer(h, moe_norm[j], moe_router[j], bf(moe_w_gate[j]), bf(moe_w_up[j]), bf(moe_w_down[j]))
    return final_rmsnorm(h.reshape(batch, tp, d), final_norm)
```

```python
import functools

import jax
import jax.numpy as jnp
from jax import lax
from jax.experimental import pallas as pl
from jax.experimental.pallas import tpu as pltpu

F32 = jnp.float32
BF16 = jnp.bfloat16

N_META = 16
EPS = 1e-6
LANE = 128
FRONT = 128
N_DUMMY = FRONT - N_META
ML_HEADS = 8
ML_DQK = 128
ML_DV = 256
ML_CHUNK = 128
GATE_CAP = 15.0
NEG_BIG = -1e30
FOX_HEADS = 32
FOX_DH = 64
FOX_PAIRS = FOX_HEADS // 2
FOX_ROWS = 16
LOG2E = 1.4426950408889634
N_EXPERTS = 8
TOP_K = 2

VMEM_LIMIT = 56 * 1024 * 1024


def _params(sem, vmem=VMEM_LIMIT):
    return pltpu.CompilerParams(dimension_semantics=sem, vmem_limit_bytes=vmem)


def _tile(n, pref):
    best = None
    t = LANE
    while t <= min(n, pref):
        if n % t == 0:
            best = t
        t += LANE
    assert best is not None, (n, pref)
    return best


def _split3(x):
    x1 = x.astype(BF16)
    r1 = x - x1.astype(F32)
    x2 = r1.astype(BF16)
    r2 = r1 - x2.astype(F32)
    return x1, x2, r2.astype(BF16)


def _log_sigmoid(x):
    return -(jnp.maximum(-x, 0.0) + jnp.log1p(jnp.exp(-jnp.abs(x))))


def _sigmoid(x):
    return 1.0 / (1.0 + jnp.exp(-x))


def _rmsnorm_kernel(x_ref, g_ref, o_ref):
    x = x_ref[...]
    ms = jnp.mean(x * x, axis=-1, keepdims=True)
    o_ref[...] = (x * lax.rsqrt(ms + EPS) * g_ref[...]).astype(o_ref.dtype)


def rmsnorm(x, g, out_dtype):
    n, d = x.shape
    tm = _tile(n, 512)
    return pl.pallas_call(
        _rmsnorm_kernel,
        grid=(n // tm,),
        in_specs=[pl.BlockSpec((tm, d), lambda i: (i, 0)),
                  pl.BlockSpec((1, d), lambda i: (0, 0))],
        out_specs=pl.BlockSpec((tm, d), lambda i: (i, 0)),
        out_shape=jax.ShapeDtypeStruct((n, d), out_dtype),
        compiler_params=_params(("parallel",)),
        name="rmsnorm",
    )(x, g.reshape(1, d))


def _final_norm_kernel(x_ref, g_ref, o_ref):
    x = x_ref[0]
    ms = jnp.mean(x * x, axis=-1, keepdims=True)
    o_ref[0] = x * lax.rsqrt(ms + EPS) * g_ref[...]


def final_rmsnorm(h3, g):
    b, tp, d = h3.shape
    nblk = (tp - FRONT) // LANE
    return pl.pallas_call(
        _final_norm_kernel,
        grid=(b, nblk),
        in_specs=[pl.BlockSpec((1, LANE, d), lambda i, j: (i, j + FRONT // LANE, 0)),
                  pl.BlockSpec((1, d), lambda i, j: (0, 0))],
        out_specs=pl.BlockSpec((1, LANE, d), lambda i, j: (i, j, 0)),
        out_shape=jax.ShapeDtypeStruct((b, tp - FRONT, d), F32),
        compiler_params=_params(("parallel", "parallel")),
        name="final_norm",
    )(h3, g.reshape(1, d))


def _group_sum_matrix():
    r = lax.broadcasted_iota(jnp.int32, (LANE, LANE), 0) // FOX_DH
    c = lax.broadcasted_iota(jnp.int32, (LANE, LANE), 1) // FOX_DH
    return jnp.where(r == c, 1.0, 0.0).astype(BF16)


def _matmul_kernel(*refs, has_resid, n_norm_tiles):
    x_ref, w_ref = refs[0], refs[1]
    k = 2
    gain_ref = resid_ref = None
    if n_norm_tiles:
        gain_ref = refs[k]; k += 1
    if has_resid:
        resid_ref = refs[k]; k += 1
    o_ref, wb_ref = refs[k], refs[k + 1]

    @pl.when(pl.program_id(1) == 0)
    def _():
        wb_ref[...] = w_ref[0].astype(BF16)

    acc = jnp.dot(x_ref[...], wb_ref[...], preferred_element_type=F32)
    if has_resid:
        acc = resid_ref[...] + acc

    if not n_norm_tiles:
        o_ref[...] = acc.astype(o_ref.dtype)
        return

    j = pl.program_id(0)

    @pl.when(j >= n_norm_tiles)
    def _():
        o_ref[...] = acc.astype(o_ref.dtype)

    @pl.when(j < n_norm_tiles)
    def _():
        gmat = _group_sum_matrix()
        tn = acc.shape[1]
        for c in range(tn // LANE):
            y = acc[:, c * LANE:(c + 1) * LANE]
            y2 = y * y
            hi = y2.astype(BF16)
            lo = (y2 - hi.astype(F32)).astype(BF16)
            ss = (jnp.dot(hi, gmat, preferred_element_type=F32)
                  + jnp.dot(lo, gmat, preferred_element_type=F32))
            yn = y * lax.rsqrt(ss * (1.0 / FOX_DH) + EPS) * gain_ref[:, c * LANE:(c + 1) * LANE]
            o_ref[:, c * LANE:(c + 1) * LANE] = yn.astype(o_ref.dtype)


def matmul(x, w, layer, n_cols, out_dtype, resid=None, head_gain=None, n_norm_cols=0, tm_pref=512, tn_pref=1024):
    m, kdim = x.shape
    tm = _tile(m, tm_pref)
    tn = _tile(n_cols, tn_pref)
    n_norm_tiles = n_norm_cols // tn
    assert n_norm_tiles * tn == n_norm_cols
    in_specs = [pl.BlockSpec((tm, kdim), lambda j, i: (i, 0)),
                pl.BlockSpec((1, kdim, tn), lambda j, i: (layer, 0, j))]
    args = [x, w]
    if n_norm_tiles:
        in_specs.append(pl.BlockSpec((1, tn), lambda j, i: (0, j)))
        args.append(head_gain)
    if resid is not None:
        in_specs.append(pl.BlockSpec((tm, tn), lambda j, i: (i, j)))
        args.append(resid)
    return pl.pallas_call(
        functools.partial(_matmul_kernel, has_resid=resid is not None, n_norm_tiles=n_norm_tiles),
        grid=(n_cols // tn, m // tm),
        in_specs=in_specs,
        out_specs=pl.BlockSpec((tm, tn), lambda j, i: (i, j)),
        out_shape=jax.ShapeDtypeStruct((m, n_cols), out_dtype),
        scratch_shapes=[pltpu.VMEM((kdim, tn), BF16)],
        compiler_params=_params(("arbitrary", "arbitrary")),
        name="matmul",
    )(*args)


def _gate_proj_kernel(w_ref, x_ref, o_ref):
    o_ref[...] = lax.dot_general(w_ref[...], x_ref[...], (((1,), (1,)), ((), ())),
                                 preferred_element_type=F32)


def gate_proj(x, w_t):
    m, kdim = x.shape
    g = w_t.shape[0]
    tm = _tile(m, 512)
    return pl.pallas_call(
        _gate_proj_kernel,
        grid=(m // tm,),
        in_specs=[pl.BlockSpec((g, kdim), lambda i: (0, 0)),
                  pl.BlockSpec((tm, kdim), lambda i: (i, 0))],
        out_specs=pl.BlockSpec((g, tm), lambda i: (0, i)),
        out_shape=jax.ShapeDtypeStruct((g, m), F32),
        compiler_params=_params(("parallel",)),
        name="gate_proj",
    )(w_t, x)


def _mlstm_kernel(bi_ref, bf_ref, q_ref, k_ref, v_ref, og_ref, ig_ref, fg_ref, gain_ref,
                  out_ref, c_ref, m_ref):
    L = ML_CHUNK
    hd = pl.program_id(1)
    ci = pl.program_id(2)

    @pl.when(ci == 0)
    def _():
        c_ref[...] = jnp.zeros_like(c_ref)
        m_ref[...] = jnp.zeros_like(m_ref)

    pos = ci * L + lax.broadcasted_iota(jnp.int32, (1, L), 1)
    dummy = pos < N_DUMMY
    ig = ig_ref[0, 0] + bi_ref[hd]
    fg = fg_ref[0, 0] + bf_ref[hd]
    li = jnp.where(dummy, NEG_BIG, GATE_CAP * jnp.tanh(ig / GATE_CAP))
    lf = jnp.where(dummy, 0.0, _log_sigmoid(GATE_CAP * jnp.tanh(fg / GATE_CAP)))

    t_i = lax.broadcasted_iota(jnp.int32, (L, L), 0)
    s_i = lax.broadcasted_iota(jnp.int32, (L, L), 1)
    tril = s_i <= t_i
    b_col = jnp.sum(jnp.where(tril, jnp.broadcast_to(lf, (L, L)), 0.0), axis=1, keepdims=True)
    b_t = jnp.broadcast_to(b_col, (L, L))
    b_s = b_t.T
    d_log = jnp.where(tril, b_t - b_s + li, -jnp.inf)
    m_prev = m_ref[0:1, 0:1]
    inter = b_col + m_prev
    m_t = jnp.maximum(inter, jnp.max(d_log, axis=1, keepdims=True))

    q = q_ref[0]
    k = k_ref[0]
    scale = ML_DQK ** -0.5
    s = lax.dot_general(q, k, (((1,), (1,)), ((), ())), preferred_element_type=F32) * scale
    p = (s * jnp.exp(d_log - m_t)).astype(BF16)
    w_inter = jnp.exp(inter - m_t)

    lane = lax.broadcasted_iota(jnp.int32, (L, LANE), 1)
    v_ext = jnp.concatenate([v_ref[0], jnp.where(lane == 0, 1.0, 0.0).astype(BF16)], axis=1)
    c_prev = c_ref[...]
    num_ext = (jnp.dot(p, v_ext, preferred_element_type=F32)
               + w_inter * jnp.dot(q, c_prev.astype(BF16), preferred_element_type=F32))
    num = num_ext[:, :ML_DV]
    den = num_ext[:, ML_DV:ML_DV + 1]
    hh = num / jnp.maximum(jnp.abs(den), jnp.exp(-m_t))
    ms = jnp.mean(hh * hh, axis=1, keepdims=True)
    hn = hh * lax.rsqrt(ms + EPS) * gain_ref[...]
    out_ref[0] = (_sigmoid(og_ref[0].astype(F32)) * hn).astype(out_ref.dtype)

    g_tot = jnp.sum(lf, axis=1, keepdims=True)
    a = g_tot - b_s[0:1, :] + li
    m_loc = jnp.max(a, axis=1, keepdims=True)
    w_loc = jnp.exp(a - m_loc)
    k_t = (k.astype(F32) * scale).T
    c_loc = jnp.dot((k_t * w_loc).astype(BF16), v_ext, preferred_element_type=F32)
    m_new = jnp.maximum(g_tot + m_prev, m_loc)
    sp = jnp.exp(g_tot + m_prev - m_new)
    sl = jnp.exp(m_loc - m_new)
    c_ref[...] = sp * c_prev + sl * c_loc
    m_ref[...] = jnp.broadcast_to(m_new, m_ref.shape)


def mlstm_core(proj, ig, fg, b_i, b_f, h_gain, batch, tp):
    L = ML_CHUNK
    proj3 = proj.reshape(batch, tp, proj.shape[1])
    nqk = ML_HEADS
    nv = (2 * ML_HEADS * ML_DQK) // ML_DV
    smem = pl.BlockSpec(memory_space=pltpu.SMEM)
    out = pl.pallas_call(
        _mlstm_kernel,
        grid=(batch, ML_HEADS, tp // L),
        in_specs=[smem, smem,
                  pl.BlockSpec((1, L, ML_DQK), lambda b, h, c: (b, c, h)),
                  pl.BlockSpec((1, L, ML_DQK), lambda b, h, c: (b, c, nqk + h)),
                  pl.BlockSpec((1, L, ML_DV), lambda b, h, c: (b, c, nv + h)),
                  pl.BlockSpec((1, L, ML_DV), lambda b, h, c: (b, c, nv + ML_HEADS + h)),
                  pl.BlockSpec((1, 1, 1, L), lambda b, h, c: (b, h, 0, c)),
                  pl.BlockSpec((1, 1, 1, L), lambda b, h, c: (b, h, 0, c)),
                  pl.BlockSpec((1, ML_DV), lambda b, h, c: (0, h))],
        out_specs=pl.BlockSpec((1, L, ML_DV), lambda b, h, c: (b, c, h)),
        out_shape=jax.ShapeDtypeStruct((batch, tp, ML_HEADS * ML_DV), BF16),
        scratch_shapes=[pltpu.VMEM((ML_DQK, ML_DV + LANE), F32), pltpu.VMEM((8, LANE), F32)],
        compiler_params=_params(("parallel", "parallel", "arbitrary")),
        name="mlstm_core",
    )(b_i, b_f, proj3, proj3, proj3, proj3, ig, fg, h_gain.reshape(1, -1))
    return out.reshape(batch * tp, ML_HEADS * ML_DV)


def _fox_cumsum_kernel(fg_ref, bf_ref, o_ref, carry_ref):
    ci = pl.program_id(1)

    @pl.when(ci == 0)
    def _():
        carry_ref[...] = jnp.zeros_like(carry_ref)

    pos = ci * LANE + lax.broadcasted_iota(jnp.int32, (LANE, 1), 0)
    x = jnp.where(pos < N_DUMMY, 0.0, LOG2E * _log_sigmoid(fg_ref[0] + bf_ref[...]))
    r = lax.broadcasted_iota(jnp.int32, (LANE, LANE), 0)
    c = lax.broadcasted_iota(jnp.int32, (LANE, LANE), 1)
    lower = jnp.where(c <= r, 1.0, 0.0).astype(BF16)
    x1, x2, x3 = _split3(x)
    cs = (jnp.dot(lower, x1, preferred_element_type=F32)
          + jnp.dot(lower, x2, preferred_element_type=F32)
          + jnp.dot(lower, x3, preferred_element_type=F32)) + carry_ref[0:1, :]
    carry_ref[...] = jnp.broadcast_to(cs[LANE - 1:LANE, :], carry_ref.shape)
    for h in range(FOX_HEADS):
        o_ref[0, h] = jnp.broadcast_to(cs[:, h:h + 1], (LANE, LANE))


def fox_cumsum(fg, b_f, batch, tp):
    fg3 = fg.reshape(batch, tp, LANE)
    bias = jnp.zeros((1, LANE), F32).at[0, :FOX_HEADS].set(b_f)
    return pl.pallas_call(
        _fox_cumsum_kernel,
        grid=(batch, tp // LANE),
        in_specs=[pl.BlockSpec((1, LANE, LANE), lambda i, c: (i, c, 0)),
                  pl.BlockSpec((1, LANE), lambda i, c: (0, 0))],
        out_specs=pl.BlockSpec((1, FOX_HEADS, LANE, LANE), lambda i, c: (i, 0, c, 0)),
        out_shape=jax.ShapeDtypeStruct((batch, FOX_HEADS, tp, LANE), F32),
        scratch_shapes=[pltpu.VMEM((8, LANE), F32)],
        compiler_params=_params(("parallel", "arbitrary")),
        name="fox_cumsum",
    )(fg3, bias)


def _fox_attn_kernel(q_ref, k_ref, v_ref, og_ref, cb_ref, o_ref,
                     qm_ref, vt_ref, ka_ref, s_ref, m_ref, l_ref, acc_ref, *, blk, nblk):
    qi = pl.program_id(2)
    nt = (((1,), (1,)), ((), ()))
    sub = blk // LANE
    lane = lax.broadcasted_iota(jnp.int32, (1, LANE), 1)
    lo = lane < FOX_DH
    bias_lane = (FOX_DH, 0)

    @pl.when(qi == 0)
    def _():
        row = lax.broadcasted_iota(jnp.int32, (LANE, 1), 0)
        for j in range(nblk * sub):
            rows = slice(j * LANE, (j + 1) * LANE)
            dst = (j // sub, slice((j % sub) * LANE, (j % sub + 1) * LANE))
            vt_ref[dst[0], :, dst[1]] = v_ref[0, rows, :].astype(F32).T.astype(BF16)
            kt = k_ref[0, rows, :].astype(F32)
            for hh in range(2):
                cb = cb_ref[0, hh, rows, :]
                if j == 0:
                    cb = jnp.where(row < N_DUMMY, -NEG_BIG, cb)
                c1, c2, c3 = (t.astype(F32) for t in _split3(cb))
                b0 = bias_lane[hh]
                aug = jnp.where(lane == b0, c1, jnp.where(lane == b0 + 1, c2,
                                jnp.where(lane == b0 + 2, c3, 0.0)))
                ka = jnp.where(lo if hh == 0 else jnp.logical_not(lo), kt, aug)
                ka_ref[hh, dst[0], dst[1], :] = ka.astype(BF16)

    q = q_ref[0].astype(F32)
    qm_ref[0] = jnp.where(lo, q, jnp.where((lane >= bias_lane[0]) & (lane < bias_lane[0] + 3), -1.0, 0.0)).astype(BF16)
    qm_ref[1] = jnp.where(lo, jnp.where(lane < bias_lane[1] + 3, -1.0, 0.0), q).astype(BF16)
    m_ref[...] = jnp.full_like(m_ref, -jnp.inf)
    l_ref[...] = jnp.zeros_like(l_ref)
    acc_ref[...] = jnp.zeros_like(acc_ref)

    def scores(ki, slot):
        for hh in range(2):
            s_ref[slot, hh] = lax.dot_general(ka_ref[hh, ki], qm_ref[hh], nt, preferred_element_type=F32)

    def softmax_pv(ki, slot, diagonal):
        vt = vt_ref[ki]
        if diagonal:
            spos = lax.broadcasted_iota(jnp.int32, (blk, 1), 0)
            tpos = lax.broadcasted_iota(jnp.int32, (1, blk), 1)
            off = ki * blk
            valid = (spos <= tpos) & ((spos + off >= N_DUMMY) | (tpos + off < N_DUMMY))
        for hh in range(2):
            s = s_ref[slot, hh]
            if diagonal:
                s = jnp.where(valid, s, -jnp.inf)
            m_prev = m_ref[hh]
            m_new = jnp.maximum(m_prev, jnp.max(s, axis=0, keepdims=True))
            alpha = jnp.exp2(m_prev - m_new)
            p = jnp.exp2(s - m_new)
            l_ref[hh] = alpha * l_ref[hh] + jnp.sum(p, axis=0, keepdims=True)
            m_ref[hh] = m_new
            pv = jnp.dot(vt[hh * FOX_DH:(hh + 1) * FOX_DH, :], p.astype(BF16), preferred_element_type=F32)
            acc_ref[hh] = acc_ref[hh] * alpha + pv

    scores(0, 0)

    def body(i, carry):
        scores(2 * i + 1, 1)
        softmax_pv(2 * i, 0, False)
        scores(2 * i + 2, 0)
        softmax_pv(2 * i + 1, 1, False)
        return carry

    lax.fori_loop(0, qi // 2, body, 0)

    @pl.when(qi % 2 == 0)
    def _():
        softmax_pv(qi, 0, True)

    @pl.when(qi % 2 == 1)
    def _():
        scores(qi, 1)
        softmax_pv(qi - 1, 0, False)
        softmax_pv(qi, 1, True)

    y_t = jnp.concatenate([acc_ref[0] / l_ref[0], acc_ref[1] / l_ref[1]], axis=0)
    y = jnp.concatenate([y_t[:, j * LANE:(j + 1) * LANE].T for j in range(sub)], axis=0)
    o_ref[0] = (_sigmoid(og_ref[0].astype(F32)) * y).astype(o_ref.dtype)


def fox_attention(proj, cb, batch, tp):
    d = FOX_HEADS * FOX_DH
    blk = _tile(tp, 512)
    nblk = tp // blk
    proj3 = proj.reshape(batch, tp, proj.shape[1])
    out = pl.pallas_call(
        functools.partial(_fox_attn_kernel, blk=blk, nblk=nblk),
        grid=(batch, FOX_PAIRS, nblk),
        in_specs=[pl.BlockSpec((1, blk, LANE), lambda b, p, qi: (b, qi, p)),
                  pl.BlockSpec((1, tp, LANE), lambda b, p, qi: (b, 0, FOX_PAIRS + p)),
                  pl.BlockSpec((1, tp, LANE), lambda b, p, qi: (b, 0, 2 * FOX_PAIRS + p)),
                  pl.BlockSpec((1, blk, LANE), lambda b, p, qi: (b, qi, 3 * FOX_PAIRS + p)),
                  pl.BlockSpec((1, 2, tp, LANE), lambda b, p, qi: (b, p, 0, 0))],
        out_specs=pl.BlockSpec((1, blk, LANE), lambda b, p, qi: (b, qi, p)),
        out_shape=jax.ShapeDtypeStruct((batch, tp, d), BF16),
        scratch_shapes=[pltpu.VMEM((2, blk, LANE), BF16),
                        pltpu.VMEM((nblk, LANE, blk), BF16),
                        pltpu.VMEM((2, nblk, blk, LANE), BF16),
                        pltpu.VMEM((2, 2, blk, blk), F32),
                        pltpu.VMEM((2, 1, blk), F32), pltpu.VMEM((2, 1, blk), F32),
                        pltpu.VMEM((2, FOX_DH, blk), F32)],
        compiler_params=_params(("arbitrary", "arbitrary", "arbitrary")),
        name="fox_attention",
    )(proj3, proj3, proj3, proj3, cb)
    return out.reshape(batch * tp, d)


def _ffn_kernel(te_ref, nu_ref, x_ref, g_ref, wg_ref, wu_ref, wd_ref, *rest, has_resid, n_ff):
    if has_resid:
        resid_ref, o_ref, xb_ref, act_ref = rest
    else:
        (o_ref, xb_ref, act_ref), resid_ref = rest, None
    i = pl.program_id(0)
    f = pl.program_id(1)

    @pl.when(i < nu_ref[0])
    def _():
        def gate_up():
            xb = xb_ref[...]
            gate = jnp.dot(xb, wg_ref[0].astype(BF16), preferred_element_type=F32)
            up = jnp.dot(xb, wu_ref[0].astype(BF16), preferred_element_type=F32)
            return (gate * _sigmoid(gate) * up).astype(BF16)

        def down():
            return jnp.dot(act_ref[...], wd_ref[0].astype(BF16), preferred_element_type=F32)

        @pl.when(f == 0)
        def _():
            x = x_ref[...]
            ms = jnp.mean(x * x, axis=-1, keepdims=True)
            xb_ref[...] = (x * lax.rsqrt(ms + EPS) * g_ref[...]).astype(BF16)
            o_ref[...] = resid_ref[...] if has_resid else jnp.zeros_like(o_ref)
            act_ref[...] = gate_up()

        @pl.when((f > 0) & (f < n_ff))
        def _():
            contrib = down()
            act = gate_up()
            o_ref[...] += contrib
            act_ref[...] = act

        @pl.when(f == n_ff)
        def _():
            o_ref[...] += down()

    @pl.when((i >= nu_ref[0]) & (f == 0))
    def _():
        o_ref[...] = jnp.zeros_like(o_ref)


def grouped_ffn(x, norm_g, w_gate, w_up, w_down, tile_expert, n_used, tm, tf, resid=None):
    r, d = x.shape
    ff = w_gate.shape[2]
    n_tiles = r // tm
    n_ff = ff // tf
    assert n_tiles * tm == r and n_ff * tf == ff

    def row_map(i, f, te, nu):
        return (jnp.minimum(i, nu[0] - 1), 0)

    def _ef(i, f, te, nu):
        ii = jnp.minimum(i, nu[0] - 1)
        return te[ii], jnp.where(i < nu[0], f, n_ff)

    def wcol_map(i, f, te, nu):
        e, fi = _ef(i, f, te, nu)
        return (e, 0, jnp.minimum(fi, n_ff - 1))

    def wrow_map(i, f, te, nu):
        e, fi = _ef(i, f, te, nu)
        return (e, jnp.maximum(fi - 1, 0), 0)

    in_specs = [pl.BlockSpec((tm, d), row_map),
                pl.BlockSpec((1, d), lambda i, f, te, nu: (0, 0)),
                pl.BlockSpec((1, d, tf), wcol_map),
                pl.BlockSpec((1, d, tf), wcol_map),
                pl.BlockSpec((1, tf, d), wrow_map)]
    args = [x, norm_g.reshape(1, d), w_gate, w_up, w_down]
    if resid is not None:
        in_specs.append(pl.BlockSpec((tm, d), row_map))
        args.append(resid)
    return pl.pallas_call(
        functools.partial(_ffn_kernel, has_resid=resid is not None, n_ff=n_ff),
        grid_spec=pltpu.PrefetchScalarGridSpec(
            num_scalar_prefetch=2,
            grid=(n_tiles, n_ff + 1),
            in_specs=in_specs,
            out_specs=pl.BlockSpec((tm, d), lambda i, f, te, nu: (i, 0)),
            scratch_shapes=[pltpu.VMEM((tm, d), BF16), pltpu.VMEM((tm, tf), BF16)]),
        out_shape=jax.ShapeDtypeStruct((r, d), F32),
        compiler_params=_params(("arbitrary", "arbitrary")),
        name="grouped_ffn",
    )(tile_expert, n_used, *args)


def _router_kernel(x_ref, g_ref, rt_ref, idx_ref, gate_ref, rank_ref, cnt_ref, base_ref, *, tm):
    i = pl.program_id(0)

    @pl.when(i == 0)
    def _():
        base_ref[...] = jnp.zeros_like(base_ref)

    x = x_ref[...]
    ms = jnp.mean(x * x, axis=-1, keepdims=True)
    hn = x * lax.rsqrt(ms + EPS) * g_ref[...]
    r1, r2, _ = _split3(rt_ref[...])
    h1, h2, _ = _split3(hn)
    nt = (((1,), (1,)), ((), ()))
    logits = (lax.dot_general(r1, h1, nt, preferred_element_type=F32)
              + lax.dot_general(r1, h2, nt, preferred_element_type=F32)
              + lax.dot_general(r2, h1, nt, preferred_element_type=F32))

    e_i = lax.broadcasted_iota(jnp.int32, (N_EXPERTS, tm), 0)
    v1 = jnp.max(logits, axis=0, keepdims=True)
    i1 = jnp.min(jnp.where(logits == v1, e_i, N_EXPERTS), axis=0, keepdims=True)
    sel1 = e_i == i1
    rest = jnp.where(sel1, -jnp.inf, logits)
    v2 = jnp.max(rest, axis=0, keepdims=True)
    i2 = jnp.min(jnp.where(rest == v2, e_i, N_EXPERTS), axis=0, keepdims=True)
    sel2 = e_i == i2
    e2 = jnp.exp(v2 - v1)
    den = 1.0 + e2
    idx_ref[...] = jnp.concatenate([i1, i2], axis=0)
    gate_ref[...] = jnp.concatenate([1.0 / den, e2 / den], axis=0)

    sel = jnp.where(sel1 | sel2, 1.0, 0.0)
    t_r = lax.broadcasted_iota(jnp.int32, (tm, tm), 0)
    t_c = lax.broadcasted_iota(jnp.int32, (tm, tm), 1)
    before = jnp.where(t_r < t_c, 1.0, 0.0).astype(BF16)
    tot = base_ref[:, 0:1] + jnp.dot(sel.astype(BF16), before, preferred_element_type=F32)
    rk1 = jnp.sum(jnp.where(sel1, tot, 0.0), axis=0, keepdims=True)
    rk2 = jnp.sum(jnp.where(sel2, tot, 0.0), axis=0, keepdims=True)
    rank_ref[...] = jnp.concatenate([rk1, rk2], axis=0).astype(jnp.int32)
    new_base = base_ref[...] + jnp.sum(sel, axis=1, keepdims=True)
    base_ref[...] = new_base
    cnt_ref[...] = new_base.astype(jnp.int32)


def router(h, norm_g, router_w):
    n, d = h.shape
    tm = _tile(n, 512)
    row2 = lambda i: (0, i)
    idx, gate, rank, cnt = pl.pallas_call(
        functools.partial(_router_kernel, tm=tm),
        grid=(n // tm,),
        in_specs=[pl.BlockSpec((tm, d), lambda i: (i, 0)),
                  pl.BlockSpec((1, d), lambda i: (0, 0)),
                  pl.BlockSpec((N_EXPERTS, d), lambda i: (0, 0))],
        out_specs=[pl.BlockSpec((TOP_K, tm), row2), pl.BlockSpec((TOP_K, tm), row2),
                   pl.BlockSpec((TOP_K, tm), row2), pl.BlockSpec((N_EXPERTS, LANE), lambda i: (0, 0))],
        out_shape=[jax.ShapeDtypeStruct((TOP_K, n), jnp.int32), jax.ShapeDtypeStruct((TOP_K, n), F32),
                   jax.ShapeDtypeStruct((TOP_K, n), jnp.int32), jax.ShapeDtypeStruct((N_EXPERTS, LANE), jnp.int32)],
        scratch_shapes=[pltpu.VMEM((N_EXPERTS, LANE), F32)],
        compiler_params=_params(("arbitrary",)),
        name="router",
    )(h, norm_g.reshape(1, d), router_w.T)
    return idx, gate, rank, cnt[:, 0]


def _row_copy(src_ref, s, dst_ref, t, sem):
    return pltpu.make_async_copy(src_ref.at[pl.ds(s, 1), :], dst_ref.at[pl.ds(t, 1), :], sem)


def _dispatch_kernel(pos_ref, h_ref, xs_in_ref, xs_ref, sem, *, tt):
    del xs_in_ref

    def start(j, carry):
        for kk in range(TOP_K):
            _row_copy(h_ref, j, xs_ref, pos_ref[0, kk, j], sem).start()
        return carry

    def wait(j, carry):
        for kk in range(TOP_K):
            _row_copy(h_ref, j, xs_ref, pos_ref[0, kk, j], sem).wait()
        return carry

    lax.fori_loop(0, tt, start, 0, unroll=8)
    lax.fori_loop(0, tt, wait, 0, unroll=8)


def dispatch(h, pos_tiles, n_rows, tt):
    n, d = h.shape
    xs0 = jnp.zeros((n_rows, d), h.dtype)
    any_spec = pl.BlockSpec(memory_space=pl.ANY)
    return pl.pallas_call(
        functools.partial(_dispatch_kernel, tt=tt),
        grid=(n // tt,),
        in_specs=[pl.BlockSpec((1, TOP_K, tt), lambda i: (i, 0, 0), memory_space=pltpu.SMEM),
                  pl.BlockSpec((tt, d), lambda i: (i, 0)), any_spec],
        out_specs=any_spec,
        out_shape=jax.ShapeDtypeStruct((n_rows, d), h.dtype),
        scratch_shapes=[pltpu.SemaphoreType.DMA],
        input_output_aliases={2: 0},
        compiler_params=_params(("arbitrary",)),
        name="moe_dispatch",
    )(pos_tiles, h, xs0)


def _combine_kernel(pos_ref, h_ref, gate_ref, ys_ref, o_ref, buf_ref, sem, *, tt):
    def start(j, carry):
        for kk in range(TOP_K):
            _row_copy(ys_ref, pos_ref[0, kk, j], buf_ref.at[kk], j, sem).start()
        return carry

    def wait(j, carry):
        for kk in range(TOP_K):
            _row_copy(ys_ref, pos_ref[0, kk, j], buf_ref.at[kk], j, sem).wait()
        return carry

    lax.fori_loop(0, tt, start, 0)
    lax.fori_loop(0, tt, wait, 0)
    g = gate_ref[...]
    o_ref[...] = h_ref[...] + (g[:, 0:1] * buf_ref[0] + g[:, 1:2] * buf_ref[1])


def combine(h, gate_cols, ys, pos_tiles, tt):
    n, d = h.shape
    return pl.pallas_call(
        functools.partial(_combine_kernel, tt=tt),
        grid=(n // tt,),
        in_specs=[pl.BlockSpec((1, TOP_K, tt), lambda i: (i, 0, 0), memory_space=pltpu.SMEM),
                  pl.BlockSpec((tt, d), lambda i: (i, 0)),
                  pl.BlockSpec((tt, TOP_K), lambda i: (i, 0)),
                  pl.BlockSpec(memory_space=pl.ANY)],
        out_specs=pl.BlockSpec((tt, d), lambda i: (i, 0)),
        out_shape=jax.ShapeDtypeStruct((n, d), F32),
        scratch_shapes=[pltpu.VMEM((TOP_K, tt, d), F32), pltpu.SemaphoreType.DMA],
        compiler_params=_params(("arbitrary",)),
        name="moe_combine",
    )(pos_tiles, h, gate_cols, ys)


def moe_layer(h, norm_g, router_w, w_gate, w_up, w_down, tm=512, tf=512, tt=256):
    n, d = h.shape
    idx, gate, rank, counts = router(h, norm_g, router_w)
    tiles_e = (counts + tm - 1) // tm
    tile_end = jnp.cumsum(tiles_e)
    starts = (tile_end - tiles_e) * tm
    pos = rank
    for e in range(N_EXPERTS):
        pos = pos + jnp.where(idx == e, starts[e], 0)
    n_tiles = (TOP_K * n) // tm + N_EXPERTS
    tile_expert = jnp.minimum(jnp.searchsorted(tile_end, jnp.arange(n_tiles, dtype=jnp.int32), side="right"),
                              N_EXPERTS - 1).astype(jnp.int32)
    n_used = tile_end[-1:].astype(jnp.int32)
    tt = _tile(n, tt)
    pos_tiles = pos.reshape(TOP_K, n // tt, tt).transpose(1, 0, 2)
    xs = dispatch(h, pos_tiles, n_tiles * tm, tt)
    ys = grouped_ffn(xs, norm_g, w_gate, w_up, w_down, tile_expert, n_used, tm, tf)
    return combine(h, gate.T, ys, pos_tiles, tt)


def dense_ffn(h, norm_g, w_gate, w_up, w_down, tm=512, tf=512):
    n, d = h.shape
    tm = _tile(n, tm)
    n_tiles = n // tm
    return grouped_ffn(h, norm_g, w_gate[None], w_up[None], w_down[None],
                       jnp.zeros((n_tiles,), jnp.int32), jnp.full((1,), n_tiles, jnp.int32), tm, tf, resid=h)


def mlstm_layer(h, batch, tp, j, norm_g, w_in, b_i, b_f, h_gain, w_out):
    d = h.shape[1]
    n_main = 2 * ML_HEADS * ML_DQK + 2 * ML_HEADS * ML_DV
    hn = rmsnorm(h, norm_g, BF16)
    proj = matmul(hn, w_in, j, n_main, BF16)
    gates = gate_proj(hn, w_in[j, :, n_main:].T.astype(BF16))
    gates = gates.reshape(2, ML_HEADS, batch, tp).transpose(0, 2, 1, 3).reshape(2, batch, ML_HEADS, 1, tp)
    y = mlstm_core(proj, gates[0], gates[1], b_i, b_f, h_gain, batch, tp)
    return matmul(y, w_out, j, d, F32, resid=h)


def fox_layer(h, batch, tp, j, norm_g, w_in, b_f, q_gain, k_gain, w_out):
    d = h.shape[1]
    n_main = 4 * d
    hn = rmsnorm(h, norm_g, BF16)
    head_gain = jnp.concatenate([jnp.tile(q_gain * (FOX_DH ** -0.5 * LOG2E), FOX_HEADS), jnp.tile(k_gain, FOX_HEADS),
                                 jnp.ones((2 * d,), F32)]).reshape(1, n_main)
    proj = matmul(hn, w_in, j, n_main, BF16, head_gain=head_gain, n_norm_cols=2 * d)
    w_fg = jnp.zeros((1, d, LANE), F32).at[0, :, :FOX_HEADS].set(w_in[j, :, n_main:])
    fg = matmul(hn, w_fg, 0, LANE, F32)
    cb = fox_cumsum(fg, b_f, batch, tp)
    y = fox_attention(proj, cb, batch, tp)
    return matmul(y, w_out, j, d, F32, resid=h)


def kernel(x, meta_tokens, ml_norm, ml_w_in, ml_b_i, ml_b_f, ml_h_gain, ml_w_out, ffn_norm, ffn_w_gate, ffn_w_up, ffn_w_down, fox_norm, fox_w_in, fox_b_f, fox_q_gain, fox_k_gain, fox_w_out, moe_norm, moe_router, moe_w_gate, moe_w_up, moe_w_down, final_norm):
    batch, seq, d = x.shape
    tp = FRONT + seq
    depth = ml_norm.shape[0] + fox_norm.shape[0]
    front = jnp.concatenate([jnp.zeros((N_DUMMY, d), x.dtype), meta_tokens.astype(x.dtype)], axis=0)
    h = jnp.concatenate([jnp.broadcast_to(front[None], (batch, FRONT, d)), x], axis=1).reshape(batch * tp, d)
    bf = lambda w: w.astype(BF16)
    for i in range(depth):
        j = i // 2
        if i % 2 == 0:
            h = mlstm_layer(h, batch, tp, j, ml_norm[j], ml_w_in, ml_b_i[j], ml_b_f[j], ml_h_gain[j], ml_w_out)
            h = dense_ffn(h, ffn_norm[j], bf(ffn_w_gate[j]), bf(ffn_w_up[j]), bf(ffn_w_down[j]))
        else:
            h = fox_layer(h, batch, tp, j, fox_norm[j], fox_w_in, fox_b_f[j], fox_q_gain[j], fox_k_gain[j], fox_w_out)
            h = moe_layer(h, moe_norm[j], moe_router[j], bf(moe_w_gate[j]), bf(moe_w_up[j]), bf(moe_w_down[j]))
    return final_rmsnorm(h.reshape(batch, tp, d), final_norm)
```

```python
import functools

import jax
import jax.numpy as jnp
from jax import lax
from jax.experimental import pallas as pl
from jax.experimental.pallas import tpu as pltpu

F32 = jnp.float32
BF16 = jnp.bfloat16

N_META = 16
EPS = 1e-6
LANE = 128
FRONT = 128
N_DUMMY = FRONT - N_META
ML_HEADS = 8
ML_DQK = 128
ML_DV = 256
ML_CHUNK = 128
GATE_CAP = 15.0
NEG_BIG = -1e30
FOX_HEADS = 32
FOX_DH = 64
FOX_PAIRS = FOX_HEADS // 2
FOX_ROWS = 16
LOG2E = 1.4426950408889634
N_EXPERTS = 8
TOP_K = 2
FF_TILE = 512

VMEM_LIMIT = 56 * 1024 * 1024


def _params(sem, vmem=VMEM_LIMIT):
    return pltpu.CompilerParams(dimension_semantics=sem, vmem_limit_bytes=vmem)


def _tile(n, pref):
    best = None
    t = LANE
    while t <= min(n, pref):
        if n % t == 0:
            best = t
        t += LANE
    assert best is not None, (n, pref)
    return best


def _split3(x):
    x1 = x.astype(BF16)
    r1 = x - x1.astype(F32)
    x2 = r1.astype(BF16)
    r2 = r1 - x2.astype(F32)
    return x1, x2, r2.astype(BF16)


def _log_sigmoid(x):
    return -(jnp.maximum(-x, 0.0) + jnp.log1p(jnp.exp(-jnp.abs(x))))


def _sigmoid(x):
    return 1.0 / (1.0 + jnp.exp(-x))


def _rmsnorm_kernel(x_ref, g_ref, o_ref):
    x = x_ref[...]
    ms = jnp.mean(x * x, axis=-1, keepdims=True)
    o_ref[...] = (x * lax.rsqrt(ms + EPS) * g_ref[...]).astype(o_ref.dtype)


def rmsnorm(x, g, out_dtype):
    n, d = x.shape
    tm = _tile(n, 512)
    return pl.pallas_call(
        _rmsnorm_kernel,
        grid=(n // tm,),
        in_specs=[pl.BlockSpec((tm, d), lambda i: (i, 0)),
                  pl.BlockSpec((1, d), lambda i: (0, 0))],
        out_specs=pl.BlockSpec((tm, d), lambda i: (i, 0)),
        out_shape=jax.ShapeDtypeStruct((n, d), out_dtype),
        compiler_params=_params(("parallel",)),
        name="rmsnorm",
    )(x, g.reshape(1, d))


def _final_norm_kernel(x_ref, g_ref, o_ref):
    x = x_ref[0]
    ms = jnp.mean(x * x, axis=-1, keepdims=True)
    o_ref[0] = x * lax.rsqrt(ms + EPS) * g_ref[...]


def final_rmsnorm(h3, g):
    b, tp, d = h3.shape
    nblk = (tp - FRONT) // LANE
    return pl.pallas_call(
        _final_norm_kernel,
        grid=(b, nblk),
        in_specs=[pl.BlockSpec((1, LANE, d), lambda i, j: (i, j + FRONT // LANE, 0)),
                  pl.BlockSpec((1, d), lambda i, j: (0, 0))],
        out_specs=pl.BlockSpec((1, LANE, d), lambda i, j: (i, j, 0)),
        out_shape=jax.ShapeDtypeStruct((b, tp - FRONT, d), F32),
        compiler_params=_params(("parallel", "parallel")),
        name="final_norm",
    )(h3, g.reshape(1, d))


def _group_sum_matrix():
    r = lax.broadcasted_iota(jnp.int32, (LANE, LANE), 0) // FOX_DH
    c = lax.broadcasted_iota(jnp.int32, (LANE, LANE), 1) // FOX_DH
    return jnp.where(r == c, 1.0, 0.0).astype(BF16)


def _matmul_kernel(*refs, has_resid, n_norm_tiles, n_valid):
    x_ref, w_ref = refs[0], refs[1]
    k = 2
    gain_ref = resid_ref = None
    if n_norm_tiles:
        gain_ref = refs[k]; k += 1
    if has_resid:
        resid_ref = refs[k]; k += 1
    o_ref, wb_ref = refs[k], refs[k + 1]

    @pl.when(pl.program_id(1) == 0)
    def _():
        w = w_ref[0]
        if n_valid is not None:
            col = lax.broadcasted_iota(jnp.int32, (1, w.shape[1]), 1)
            w = jnp.where(col < n_valid, w, 0.0)
        wb_ref[...] = w.astype(BF16)

    acc = jnp.dot(x_ref[...], wb_ref[...], preferred_element_type=F32)
    if has_resid:
        acc = resid_ref[...] + acc

    if not n_norm_tiles:
        o_ref[...] = acc.astype(o_ref.dtype)
        return

    j = pl.program_id(0)

    @pl.when(j >= n_norm_tiles)
    def _():
        o_ref[...] = acc.astype(o_ref.dtype)

    @pl.when(j < n_norm_tiles)
    def _():
        gmat = _group_sum_matrix()
        tn = acc.shape[1]
        for c in range(tn // LANE):
            y = acc[:, c * LANE:(c + 1) * LANE]
            y2 = y * y
            hi = y2.astype(BF16)
            lo = (y2 - hi.astype(F32)).astype(BF16)
            ss = (jnp.dot(hi, gmat, preferred_element_type=F32)
                  + jnp.dot(lo, gmat, preferred_element_type=F32))
            yn = y * lax.rsqrt(ss * (1.0 / FOX_DH) + EPS) * gain_ref[:, c * LANE:(c + 1) * LANE]
            o_ref[:, c * LANE:(c + 1) * LANE] = yn.astype(o_ref.dtype)


def matmul(x, w, layer, n_cols, out_dtype, resid=None, head_gain=None, n_norm_cols=0, col_start=0, n_valid=None,
           tm_pref=512, tn_pref=1024):
    m, kdim = x.shape
    tm = _tile(m, tm_pref)
    tn = _tile(n_cols, tn_pref)
    n_norm_tiles = n_norm_cols // tn
    assert n_norm_tiles * tn == n_norm_cols and col_start % tn == 0
    assert n_valid is None or n_cols == tn
    col_off = col_start // tn
    in_specs = [pl.BlockSpec((tm, kdim), lambda j, i: (i, 0)),
                pl.BlockSpec((1, kdim, tn), lambda j, i: (layer, 0, j + col_off))]
    args = [x, w]
    if n_norm_tiles:
        in_specs.append(pl.BlockSpec((1, tn), lambda j, i: (0, j)))
        args.append(head_gain)
    if resid is not None:
        in_specs.append(pl.BlockSpec((tm, tn), lambda j, i: (i, j)))
        args.append(resid)
    return pl.pallas_call(
        functools.partial(_matmul_kernel, has_resid=resid is not None, n_norm_tiles=n_norm_tiles, n_valid=n_valid),
        grid=(n_cols // tn, m // tm),
        in_specs=in_specs,
        out_specs=pl.BlockSpec((tm, tn), lambda j, i: (i, j)),
        out_shape=jax.ShapeDtypeStruct((m, n_cols), out_dtype),
        scratch_shapes=[pltpu.VMEM((kdim, tn), BF16)],
        compiler_params=_params(("arbitrary", "arbitrary")),
        name="matmul",
    )(*args)


def _mlstm_kernel(bi_ref, bf_ref, q_ref, k_ref, v_ref, og_ref, ig_ref, fg_ref, gain_ref,
                  out_ref, c_ref, m_ref):
    L = ML_CHUNK
    hd = pl.program_id(1)
    ci = pl.program_id(2)

    @pl.when(ci == 0)
    def _():
        c_ref[...] = jnp.zeros_like(c_ref)
        m_ref[...] = jnp.zeros_like(m_ref)

    pos = ci * L + lax.broadcasted_iota(jnp.int32, (1, L), 1)
    dummy = pos < N_DUMMY
    ig = ig_ref[0, 0] + bi_ref[hd]
    fg = fg_ref[0, 0] + bf_ref[hd]
    li = jnp.where(dummy, NEG_BIG, GATE_CAP * jnp.tanh(ig / GATE_CAP))
    lf = jnp.where(dummy, 0.0, _log_sigmoid(GATE_CAP * jnp.tanh(fg / GATE_CAP)))

    t_i = lax.broadcasted_iota(jnp.int32, (L, L), 0)
    s_i = lax.broadcasted_iota(jnp.int32, (L, L), 1)
    tril = s_i <= t_i
    b_col = jnp.sum(jnp.where(tril, jnp.broadcast_to(lf, (L, L)), 0.0), axis=1, keepdims=True)
    b_t = jnp.broadcast_to(b_col, (L, L))
    b_s = b_t.T
    d_log = jnp.where(tril, b_t - b_s + li, -jnp.inf)
    m_prev = m_ref[0:1, 0:1]
    inter = b_col + m_prev
    m_t = jnp.maximum(inter, jnp.max(d_log, axis=1, keepdims=True))

    q = q_ref[0]
    k = k_ref[0]
    scale = ML_DQK ** -0.5
    s = lax.dot_general(q, k, (((1,), (1,)), ((), ())), preferred_element_type=F32) * scale
    p = (s * jnp.exp(d_log - m_t)).astype(BF16)
    w_inter = jnp.exp(inter - m_t)

    lane = lax.broadcasted_iota(jnp.int32, (L, LANE), 1)
    v_ext = jnp.concatenate([v_ref[0], jnp.where(lane == 0, 1.0, 0.0).astype(BF16)], axis=1)
    c_prev = c_ref[...]
    num_ext = (jnp.dot(p, v_ext, preferred_element_type=F32)
               + w_inter * jnp.dot(q, c_prev.astype(BF16), preferred_element_type=F32))
    num = num_ext[:, :ML_DV]
    den = num_ext[:, ML_DV:ML_DV + 1]
    hh = num / jnp.maximum(jnp.abs(den), jnp.exp(-m_t))
    ms = jnp.mean(hh * hh, axis=1, keepdims=True)
    hn = hh * lax.rsqrt(ms + EPS) * gain_ref[...]
    out_ref[0] = (_sigmoid(og_ref[0].astype(F32)) * hn).astype(out_ref.dtype)

    g_tot = jnp.sum(lf, axis=1, keepdims=True)
    a = g_tot - b_s[0:1, :] + li
    m_loc = jnp.max(a, axis=1, keepdims=True)
    w_loc = jnp.exp(a - m_loc)
    k_t = (k.astype(F32) * scale).T
    c_loc = jnp.dot((k_t * w_loc).astype(BF16), v_ext, preferred_element_type=F32)
    m_new = jnp.maximum(g_tot + m_prev, m_loc)
    sp = jnp.exp(g_tot + m_prev - m_new)
    sl = jnp.exp(m_loc - m_new)
    c_ref[...] = sp * c_prev + sl * c_loc
    m_ref[...] = jnp.broadcast_to(m_new, m_ref.shape)


def mlstm_core(proj, ig, fg, b_i, b_f, h_gain, batch, tp):
    L = ML_CHUNK
    proj3 = proj.reshape(batch, tp, proj.shape[1])
    nqk = ML_HEADS
    nv = (2 * ML_HEADS * ML_DQK) // ML_DV
    smem = pl.BlockSpec(memory_space=pltpu.SMEM)
    out = pl.pallas_call(
        _mlstm_kernel,
        grid=(batch, ML_HEADS, tp // L),
        in_specs=[smem, smem,
                  pl.BlockSpec((1, L, ML_DQK), lambda b, h, c: (b, c, h)),
                  pl.BlockSpec((1, L, ML_DQK), lambda b, h, c: (b, c, nqk + h)),
                  pl.BlockSpec((1, L, ML_DV), lambda b, h, c: (b, c, nv + h)),
                  pl.BlockSpec((1, L, ML_DV), lambda b, h, c: (b, c, nv + ML_HEADS + h)),
                  pl.BlockSpec((1, 1, 1, L), lambda b, h, c: (b, h, 0, c)),
                  pl.BlockSpec((1, 1, 1, L), lambda b, h, c: (b, h, 0, c)),
                  pl.BlockSpec((1, ML_DV), lambda b, h, c: (0, h))],
        out_specs=pl.BlockSpec((1, L, ML_DV), lambda b, h, c: (b, c, h)),
        out_shape=jax.ShapeDtypeStruct((batch, tp, ML_HEADS * ML_DV), BF16),
        scratch_shapes=[pltpu.VMEM((ML_DQK, ML_DV + LANE), F32), pltpu.VMEM((8, LANE), F32)],
        compiler_params=_params(("parallel", "parallel", "arbitrary")),
        name="mlstm_core",
    )(b_i, b_f, proj3, proj3, proj3, proj3, ig, fg, h_gain.reshape(1, -1))
    return out.reshape(batch * tp, ML_HEADS * ML_DV)


def _fox_cumsum_kernel(fg_ref, bf_ref, o_ref, carry_ref):
    ci = pl.program_id(1)

    @pl.when(ci == 0)
    def _():
        carry_ref[...] = jnp.zeros_like(carry_ref)

    pos = ci * LANE + lax.broadcasted_iota(jnp.int32, (LANE, 1), 0)
    x = jnp.where(pos < N_DUMMY, 0.0, LOG2E * _log_sigmoid(fg_ref[0] + bf_ref[...]))
    r = lax.broadcasted_iota(jnp.int32, (LANE, LANE), 0)
    c = lax.broadcasted_iota(jnp.int32, (LANE, LANE), 1)
    lower = jnp.where(c <= r, 1.0, 0.0).astype(BF16)
    x1, x2, x3 = _split3(x)
    cs = (jnp.dot(lower, x1, preferred_element_type=F32)
          + jnp.dot(lower, x2, preferred_element_type=F32)
          + jnp.dot(lower, x3, preferred_element_type=F32)) + carry_ref[0:1, :]
    carry_ref[...] = jnp.broadcast_to(cs[LANE - 1:LANE, :], carry_ref.shape)
    for h in range(FOX_HEADS):
        o_ref[0, h] = jnp.broadcast_to(cs[:, h:h + 1], (LANE, LANE))


def fox_cumsum(fg, b_f, batch, tp):
    fg3 = fg.reshape(batch, tp, LANE)
    bias = jnp.zeros((1, LANE), F32).at[0, :FOX_HEADS].set(b_f)
    return pl.pallas_call(
        _fox_cumsum_kernel,
        grid=(batch, tp // LANE),
        in_specs=[pl.BlockSpec((1, LANE, LANE), lambda i, c: (i, c, 0)),
                  pl.BlockSpec((1, LANE), lambda i, c: (0, 0))],
        out_specs=pl.BlockSpec((1, FOX_HEADS, LANE, LANE), lambda i, c: (i, 0, c, 0)),
        out_shape=jax.ShapeDtypeStruct((batch, FOX_HEADS, tp, LANE), F32),
        scratch_shapes=[pltpu.VMEM((8, LANE), F32)],
        compiler_params=_params(("parallel", "arbitrary")),
        name="fox_cumsum",
    )(fg3, bias)


def _fox_attn_kernel(q_ref, k_ref, v_ref, og_ref, cb_ref, o_ref,
                     qm_ref, vt_ref, ka_ref, s_ref, m_ref, l_ref, acc_ref, *, blk, nblk):
    qi = pl.program_id(2)
    nt = (((1,), (1,)), ((), ()))
    sub = blk // LANE
    lane = lax.broadcasted_iota(jnp.int32, (1, LANE), 1)
    lo = lane < FOX_DH
    bias_lane = (FOX_DH, 0)

    @pl.when(qi == 0)
    def _():
        row = lax.broadcasted_iota(jnp.int32, (LANE, 1), 0)
        for j in range(nblk * sub):
            rows = slice(j * LANE, (j + 1) * LANE)
            dst = (j // sub, slice((j % sub) * LANE, (j % sub + 1) * LANE))
            vt_ref[dst[0], :, dst[1]] = v_ref[0, rows, :].astype(F32).T.astype(BF16)
            kt = k_ref[0, rows, :].astype(F32)
            for hh in range(2):
                cb = cb_ref[0, hh, rows, :]
                if j == 0:
                    cb = jnp.where(row < N_DUMMY, -NEG_BIG, cb)
                c1, c2, c3 = (t.astype(F32) for t in _split3(cb))
                b0 = bias_lane[hh]
                aug = jnp.where(lane == b0, c1, jnp.where(lane == b0 + 1, c2,
                                jnp.where(lane == b0 + 2, c3, 0.0)))
                ka = jnp.where(lo if hh == 0 else jnp.logical_not(lo), kt, aug)
                ka_ref[hh, dst[0], dst[1], :] = ka.astype(BF16)

    q = q_ref[0].astype(F32)
    qm_ref[0] = jnp.where(lo, q, jnp.where((lane >= bias_lane[0]) & (lane < bias_lane[0] + 3), -1.0, 0.0)).astype(BF16)
    qm_ref[1] = jnp.where(lo, jnp.where(lane < bias_lane[1] + 3, -1.0, 0.0), q).astype(BF16)
    m_ref[...] = jnp.full_like(m_ref, -jnp.inf)
    l_ref[...] = jnp.zeros_like(l_ref)
    acc_ref[...] = jnp.zeros_like(acc_ref)

    def scores(ki, slot):
        for hh in range(2):
            s_ref[slot, hh] = lax.dot_general(ka_ref[hh, ki], qm_ref[hh], nt, preferred_element_type=F32)

    def softmax_pv(ki, slot, diagonal):
        vt = vt_ref[ki]
        if diagonal:
            spos = lax.broadcasted_iota(jnp.int32, (blk, 1), 0)
            tpos = lax.broadcasted_iota(jnp.int32, (1, blk), 1)
            off = ki * blk
            valid = (spos <= tpos) & ((spos + off >= N_DUMMY) | (tpos + off < N_DUMMY))
        for hh in range(2):
            s = s_ref[slot, hh]
            if diagonal:
                s = jnp.where(valid, s, -jnp.inf)
            m_prev = m_ref[hh]
            m_new = jnp.maximum(m_prev, jnp.max(s, axis=0, keepdims=True))
            alpha = jnp.exp2(m_prev - m_new)
            p = jnp.exp2(s - m_new)
            l_ref[hh] = alpha * l_ref[hh] + jnp.sum(p, axis=0, keepdims=True)
            m_ref[hh] = m_new
            pv = jnp.dot(vt[hh * FOX_DH:(hh + 1) * FOX_DH, :], p.astype(BF16), preferred_element_type=F32)
            acc_ref[hh] = acc_ref[hh] * alpha + pv

    scores(0, 0)

    def body(i, carry):
        scores(2 * i + 1, 1)
        softmax_pv(2 * i, 0, False)
        scores(2 * i + 2, 0)
        softmax_pv(2 * i + 1, 1, False)
        return carry

    lax.fori_loop(0, qi // 2, body, 0)

    @pl.when(qi % 2 == 0)
    def _():
        softmax_pv(qi, 0, True)

    @pl.when(qi % 2 == 1)
    def _():
        scores(qi, 1)
        softmax_pv(qi - 1, 0, False)
        softmax_pv(qi, 1, True)

    y_t = jnp.concatenate([acc_ref[0] / l_ref[0], acc_ref[1] / l_ref[1]], axis=0)
    y = jnp.concatenate([y_t[:, j * LANE:(j + 1) * LANE].T for j in range(sub)], axis=0)
    o_ref[0] = (_sigmoid(og_ref[0].astype(F32)) * y).astype(o_ref.dtype)


def fox_attention(proj, cb, batch, tp):
    d = FOX_HEADS * FOX_DH
    blk = _tile(tp, 512)
    nblk = tp // blk
    proj3 = proj.reshape(batch, tp, proj.shape[1])
    out = pl.pallas_call(
        functools.partial(_fox_attn_kernel, blk=blk, nblk=nblk),
        grid=(batch, FOX_PAIRS, nblk),
        in_specs=[pl.BlockSpec((1, blk, LANE), lambda b, p, qi: (b, qi, p)),
                  pl.BlockSpec((1, tp, LANE), lambda b, p, qi: (b, 0, FOX_PAIRS + p)),
                  pl.BlockSpec((1, tp, LANE), lambda b, p, qi: (b, 0, 2 * FOX_PAIRS + p)),
                  pl.BlockSpec((1, blk, LANE), lambda b, p, qi: (b, qi, 3 * FOX_PAIRS + p)),
                  pl.BlockSpec((1, 2, tp, LANE), lambda b, p, qi: (b, p, 0, 0))],
        out_specs=pl.BlockSpec((1, blk, LANE), lambda b, p, qi: (b, qi, p)),
        out_shape=jax.ShapeDtypeStruct((batch, tp, d), BF16),
        scratch_shapes=[pltpu.VMEM((2, blk, LANE), BF16),
                        pltpu.VMEM((nblk, LANE, blk), BF16),
                        pltpu.VMEM((2, nblk, blk, LANE), BF16),
                        pltpu.VMEM((2, 2, blk, blk), F32),
                        pltpu.VMEM((2, 1, blk), F32), pltpu.VMEM((2, 1, blk), F32),
                        pltpu.VMEM((2, FOX_DH, blk), F32)],
        compiler_params=_params(("arbitrary", "arbitrary", "arbitrary")),
        name="fox_attention",
    )(proj3, proj3, proj3, proj3, cb)
    return out.reshape(batch * tp, d)


def _ffn_kernel(te_ref, nu_ref, x_ref, g_ref, wg_ref, wu_ref, wd_ref, *rest, has_resid, n_ff):
    if has_resid:
        resid_ref, o_ref, xb_ref, act_ref = rest
    else:
        (o_ref, xb_ref, act_ref), resid_ref = rest, None
    i = pl.program_id(0)
    f = pl.program_id(1)

    @pl.when(i < nu_ref[0])
    def _():
        def gate_up():
            xb = xb_ref[...]
            gate = jnp.dot(xb, wg_ref[0, 0], preferred_element_type=F32)
            up = jnp.dot(xb, wu_ref[0, 0], preferred_element_type=F32)
            return (gate * _sigmoid(gate) * up).astype(BF16)

        def down():
            return jnp.dot(act_ref[...], wd_ref[0], preferred_element_type=F32)

        @pl.when(f == 0)
        def _():
            x = x_ref[...]
            ms = jnp.mean(x * x, axis=-1, keepdims=True)
            xb_ref[...] = (x * lax.rsqrt(ms + EPS) * g_ref[...]).astype(BF16)
            o_ref[...] = resid_ref[...] if has_resid else jnp.zeros_like(o_ref)
            act_ref[...] = gate_up()

        @pl.when((f > 0) & (f < n_ff))
        def _():
            contrib = down()
            act = gate_up()
            o_ref[...] += contrib
            act_ref[...] = act

        @pl.when(f == n_ff)
        def _():
            o_ref[...] += down()

    @pl.when((i >= nu_ref[0]) & (f == 0))
    def _():
        o_ref[...] = jnp.zeros_like(o_ref)


def ff_col_tiles(w, tf):
    e, d, ff = w.shape
    return w.astype(BF16).reshape(e, d, ff // tf, tf).transpose(0, 2, 1, 3)


def grouped_ffn(x, norm_g, w_gate, w_up, w_down, tile_expert, n_used, tm, resid=None):
    r, d = x.shape
    n_ff, tf = w_gate.shape[1], w_gate.shape[3]
    n_tiles = r // tm
    assert n_tiles * tm == r and n_ff * tf == w_down.shape[1]

    def row_map(i, f, te, nu):
        return (jnp.minimum(i, nu[0] - 1), 0)

    def _ef(i, f, te, nu):
        ii = jnp.minimum(i, nu[0] - 1)
        return te[ii], jnp.where(i < nu[0], f, n_ff)

    def wcol_map(i, f, te, nu):
        e, fi = _ef(i, f, te, nu)
        return (e, jnp.minimum(fi, n_ff - 1), 0, 0)

    def wrow_map(i, f, te, nu):
        e, fi = _ef(i, f, te, nu)
        return (e, jnp.maximum(fi - 1, 0), 0)

    in_specs = [pl.BlockSpec((tm, d), row_map),
                pl.BlockSpec((1, d), lambda i, f, te, nu: (0, 0)),
                pl.BlockSpec((1, 1, d, tf), wcol_map),
                pl.BlockSpec((1, 1, d, tf), wcol_map),
                pl.BlockSpec((1, tf, d), wrow_map)]
    args = [x, norm_g.reshape(1, d), w_gate, w_up, w_down]
    if resid is not None:
        in_specs.append(pl.BlockSpec((tm, d), row_map))
        args.append(resid)
    return pl.pallas_call(
        functools.partial(_ffn_kernel, has_resid=resid is not None, n_ff=n_ff),
        grid_spec=pltpu.PrefetchScalarGridSpec(
            num_scalar_prefetch=2,
            grid=(n_tiles, n_ff + 1),
            in_specs=in_specs,
            out_specs=pl.BlockSpec((tm, d), lambda i, f, te, nu: (i, 0)),
            scratch_shapes=[pltpu.VMEM((tm, d), BF16), pltpu.VMEM((tm, tf), BF16)]),
        out_shape=jax.ShapeDtypeStruct((r, d), F32),
        compiler_params=_params(("arbitrary", "arbitrary")),
        name="grouped_ffn",
    )(tile_expert, n_used, *args)


def _router_kernel(x_ref, g_ref, rt_ref, idx_ref, gate_ref, rank_ref, cnt_ref, base_ref, *, tm):
    i = pl.program_id(0)

    @pl.when(i == 0)
    def _():
        base_ref[...] = jnp.zeros_like(base_ref)

    x = x_ref[...]
    ms = jnp.mean(x * x, axis=-1, keepdims=True)
    hn = x * lax.rsqrt(ms + EPS) * g_ref[...]
    r1, r2, _ = _split3(rt_ref[...])
    h1, h2, _ = _split3(hn)
    nt = (((1,), (1,)), ((), ()))
    logits = (lax.dot_general(r1, h1, nt, preferred_element_type=F32)
              + lax.dot_general(r1, h2, nt, preferred_element_type=F32)
              + lax.dot_general(r2, h1, nt, preferred_element_type=F32))

    e_i = lax.broadcasted_iota(jnp.int32, (N_EXPERTS, tm), 0)
    v1 = jnp.max(logits, axis=0, keepdims=True)
    i1 = jnp.min(jnp.where(logits == v1, e_i, N_EXPERTS), axis=0, keepdims=True)
    sel1 = e_i == i1
    rest = jnp.where(sel1, -jnp.inf, logits)
    v2 = jnp.max(rest, axis=0, keepdims=True)
    i2 = jnp.min(jnp.where(rest == v2, e_i, N_EXPERTS), axis=0, keepdims=True)
    sel2 = e_i == i2
    e2 = jnp.exp(v2 - v1)
    den = 1.0 + e2
    idx_ref[...] = jnp.concatenate([i1, i2], axis=0)
    gate_ref[...] = jnp.concatenate([1.0 / den, e2 / den], axis=0)

    sel = jnp.where(sel1 | sel2, 1.0, 0.0)
    t_r = lax.broadcasted_iota(jnp.int32, (tm, tm), 0)
    t_c = lax.broadcasted_iota(jnp.int32, (tm, tm), 1)
    before = jnp.where(t_r < t_c, 1.0, 0.0).astype(BF16)
    tot = base_ref[:, 0:1] + jnp.dot(sel.astype(BF16), before, preferred_element_type=F32)
    rk1 = jnp.sum(jnp.where(sel1, tot, 0.0), axis=0, keepdims=True)
    rk2 = jnp.sum(jnp.where(sel2, tot, 0.0), axis=0, keepdims=True)
    rank_ref[...] = jnp.concatenate([rk1, rk2], axis=0).astype(jnp.int32)
    new_base = base_ref[...] + jnp.sum(sel, axis=1, keepdims=True)
    base_ref[...] = new_base
    cnt_ref[...] = new_base.astype(jnp.int32)


def router(h, norm_g, router_w):
    n, d = h.shape
    tm = _tile(n, 512)
    row2 = lambda i: (0, i)
    idx, gate, rank, cnt = pl.pallas_call(
        functools.partial(_router_kernel, tm=tm),
        grid=(n // tm,),
        in_specs=[pl.BlockSpec((tm, d), lambda i: (i, 0)),
                  pl.BlockSpec((1, d), lambda i: (0, 0)),
                  pl.BlockSpec((N_EXPERTS, d), lambda i: (0, 0))],
        out_specs=[pl.BlockSpec((TOP_K, tm), row2), pl.BlockSpec((TOP_K, tm), row2),
                   pl.BlockSpec((TOP_K, tm), row2), pl.BlockSpec((N_EXPERTS, LANE), lambda i: (0, 0))],
        out_shape=[jax.ShapeDtypeStruct((TOP_K, n), jnp.int32), jax.ShapeDtypeStruct((TOP_K, n), F32),
                   jax.ShapeDtypeStruct((TOP_K, n), jnp.int32), jax.ShapeDtypeStruct((N_EXPERTS, LANE), jnp.int32)],
        scratch_shapes=[pltpu.VMEM((N_EXPERTS, LANE), F32)],
        compiler_params=_params(("arbitrary",)),
        name="router",
    )(h, norm_g.reshape(1, d), router_w.T)
    return idx, gate, rank, cnt[:, 0]


def _row_copy(src_ref, s, dst_ref, t, sem):
    return pltpu.make_async_copy(src_ref.at[pl.ds(s, 1), :], dst_ref.at[pl.ds(t, 1), :], sem)


def _dispatch_kernel(pos_ref, h_ref, xs_in_ref, xs_ref, sem, *, tt):
    del xs_in_ref

    def start(j, carry):
        for kk in range(TOP_K):
            _row_copy(h_ref, j, xs_ref, pos_ref[0, kk, j], sem).start()
        return carry

    def wait(j, carry):
        for kk in range(TOP_K):
            _row_copy(h_ref, j, xs_ref, pos_ref[0, kk, j], sem).wait()
        return carry

    lax.fori_loop(0, tt, start, 0, unroll=8)
    lax.fori_loop(0, tt, wait, 0, unroll=8)


def dispatch(h, pos_tiles, n_rows, tt):
    n, d = h.shape
    xs0 = jnp.zeros((n_rows, d), h.dtype)
    any_spec = pl.BlockSpec(memory_space=pl.ANY)
    return pl.pallas_call(
        functools.partial(_dispatch_kernel, tt=tt),
        grid=(n // tt,),
        in_specs=[pl.BlockSpec((1, TOP_K, tt), lambda i: (i, 0, 0), memory_space=pltpu.SMEM),
                  pl.BlockSpec((tt, d), lambda i: (i, 0)), any_spec],
        out_specs=any_spec,
        out_shape=jax.ShapeDtypeStruct((n_rows, d), h.dtype),
        scratch_shapes=[pltpu.SemaphoreType.DMA],
        input_output_aliases={2: 0},
        compiler_params=_params(("arbitrary",)),
        name="moe_dispatch",
    )(pos_tiles, h, xs0)


def _combine_kernel(pos_ref, h_ref, gate_ref, ys_ref, o_ref, buf_ref, sem, *, tt):
    def start(j, carry):
        for kk in range(TOP_K):
            _row_copy(ys_ref, pos_ref[0, kk, j], buf_ref.at[kk], j, sem).start()
        return carry

    def wait(j, carry):
        for kk in range(TOP_K):
            _row_copy(ys_ref, pos_ref[0, kk, j], buf_ref.at[kk], j, sem).wait()
        return carry

    lax.fori_loop(0, tt, start, 0)
    lax.fori_loop(0, tt, wait, 0)
    g = gate_ref[...]
    o_ref[...] = h_ref[...] + (g[:, 0:1] * buf_ref[0] + g[:, 1:2] * buf_ref[1])


def combine(h, gate_cols, ys, pos_tiles, tt):
    n, d = h.shape
    return pl.pallas_call(
        functools.partial(_combine_kernel, tt=tt),
        grid=(n // tt,),
        in_specs=[pl.BlockSpec((1, TOP_K, tt), lambda i: (i, 0, 0), memory_space=pltpu.SMEM),
                  pl.BlockSpec((tt, d), lambda i: (i, 0)),
                  pl.BlockSpec((tt, TOP_K), lambda i: (i, 0)),
                  pl.BlockSpec(memory_space=pl.ANY)],
        out_specs=pl.BlockSpec((tt, d), lambda i: (i, 0)),
        out_shape=jax.ShapeDtypeStruct((n, d), F32),
        scratch_shapes=[pltpu.VMEM((TOP_K, tt, d), F32), pltpu.SemaphoreType.DMA],
        compiler_params=_params(("arbitrary",)),
        name="moe_combine",
    )(pos_tiles, h, gate_cols, ys)


def moe_layer(h, norm_g, router_w, w_gate, w_up, w_down, tm=512, tt=256):
    n, d = h.shape
    idx, gate, rank, counts = router(h, norm_g, router_w)
    tiles_e = (counts + tm - 1) // tm
    tile_end = jnp.cumsum(tiles_e)
    starts = (tile_end - tiles_e) * tm
    pos = rank
    for e in range(N_EXPERTS):
        pos = pos + jnp.where(idx == e, starts[e], 0)
    n_tiles = (TOP_K * n) // tm + N_EXPERTS
    tile_expert = jnp.minimum(jnp.searchsorted(tile_end, jnp.arange(n_tiles, dtype=jnp.int32), side="right"),
                              N_EXPERTS - 1).astype(jnp.int32)
    n_used = tile_end[-1:].astype(jnp.int32)
    tt = _tile(n, tt)
    pos_tiles = pos.reshape(TOP_K, n // tt, tt).transpose(1, 0, 2)
    xs = dispatch(h, pos_tiles, n_tiles * tm, tt)
    ys = grouped_ffn(xs, norm_g, w_gate, w_up, w_down, tile_expert, n_used, tm)
    return combine(h, gate.T, ys, pos_tiles, tt)


def dense_ffn(h, norm_g, w_gate, w_up, w_down, tm=512):
    n, d = h.shape
    tm = _tile(n, tm)
    n_tiles = n // tm
    return grouped_ffn(h, norm_g, w_gate, w_up, w_down,
                       jnp.zeros((n_tiles,), jnp.int32), jnp.full((1,), n_tiles, jnp.int32), tm, resid=h)


def mlstm_layer(h, batch, tp, j, norm_g, w_in, b_i, b_f, h_gain, w_out):
    d = h.shape[1]
    n_main = 2 * ML_HEADS * ML_DQK + 2 * ML_HEADS * ML_DV
    hn = rmsnorm(h, norm_g, BF16)
    proj = matmul(hn, w_in, j, n_main, BF16)
    gates = matmul(hn, w_in, j, LANE, F32, col_start=n_main, n_valid=2 * ML_HEADS)
    gates = gates[:, :2 * ML_HEADS].reshape(batch, tp, 2, ML_HEADS).transpose(2, 0, 3, 1)
    gates = gates.reshape(2, batch, ML_HEADS, 1, tp)
    y = mlstm_core(proj, gates[0], gates[1], b_i, b_f, h_gain, batch, tp)
    return matmul(y, w_out, j, d, F32, resid=h)


def fox_layer(h, batch, tp, j, norm_g, w_in, b_f, q_gain, k_gain, w_out):
    d = h.shape[1]
    n_main = 4 * d
    hn = rmsnorm(h, norm_g, BF16)
    head_gain = jnp.concatenate([jnp.tile(q_gain * (FOX_DH ** -0.5 * LOG2E), FOX_HEADS), jnp.tile(k_gain, FOX_HEADS),
                                 jnp.ones((2 * d,), F32)]).reshape(1, n_main)
    proj = matmul(hn, w_in, j, n_main, BF16, head_gain=head_gain, n_norm_cols=2 * d)
    fg = matmul(hn, w_in, j, LANE, F32, col_start=n_main, n_valid=FOX_HEADS)
    cb = fox_cumsum(fg, b_f, batch, tp)
    y = fox_attention(proj, cb, batch, tp)
    return matmul(y, w_out, j, d, F32, resid=h)


def kernel(x, meta_tokens, ml_norm, ml_w_in, ml_b_i, ml_b_f, ml_h_gain, ml_w_out, ffn_norm, ffn_w_gate, ffn_w_up, ffn_w_down, fox_norm, fox_w_in, fox_b_f, fox_q_gain, fox_k_gain, fox_w_out, moe_norm, moe_router, moe_w_gate, moe_w_up, moe_w_down, final_norm):
    batch, seq, d = x.shape
    tp = FRONT + seq
    depth = ml_norm.shape[0] + fox_norm.shape[0]
    front = jnp.concatenate([jnp.zeros((N_DUMMY, d), x.dtype), meta_tokens.astype(x.dtype)], axis=0)
    h = jnp.concatenate([jnp.broadcast_to(front[None], (batch, FRONT, d)), x], axis=1).reshape(batch * tp, d)
    for i in range(depth):
        j = i // 2
        if i % 2 == 0:
            h = mlstm_layer(h, batch, tp, j, ml_norm[j], ml_w_in, ml_b_i[j], ml_b_f[j], ml_h_gain[j], ml_w_out)
            h = dense_ffn(h, ffn_norm[j], ff_col_tiles(ffn_w_gate[j:j + 1], FF_TILE), ff_col_tiles(ffn_w_up[j:j + 1], FF_TILE),
                          ffn_w_down[j:j + 1].astype(BF16))
        else:
            h = fox_layer(h, batch, tp, j, fox_norm[j], fox_w_in, fox_b_f[j], fox_q_gain[j], fox_k_gain[j], fox_w_out)
            h = moe_layer(h, moe_norm[j], moe_router[j], ff_col_tiles(moe_w_gate[j], FF_TILE),
                          ff_col_tiles(moe_w_up[j], FF_TILE), moe_w_down[j].astype(BF16))
    return final_rmsnorm(h.reshape(batch, tp, d), final_norm)
```

```python
import functools

import jax
import jax.numpy as jnp
from jax import lax
from jax.experimental import pallas as pl
from jax.experimental.pallas import tpu as pltpu

F32 = jnp.float32
BF16 = jnp.bfloat16

N_META = 16
EPS = 1e-6
LANE = 128
FRONT = 128
N_DUMMY = FRONT - N_META
ML_HEADS = 8
ML_DQK = 128
ML_DV = 256
ML_CHUNK = 128
ML_GROUP = 4
GATE_CAP = 15.0
NEG_BIG = -1e30
FOX_HEADS = 32
FOX_DH = 64
FOX_PAIRS = FOX_HEADS // 2
FOX_ROWS = 16
LOG2E = 1.4426950408889634
N_EXPERTS = 8
TOP_K = 2
FF_TILE = 512

VMEM_LIMIT = 56 * 1024 * 1024


def _params(sem, vmem=VMEM_LIMIT):
    return pltpu.CompilerParams(dimension_semantics=sem, vmem_limit_bytes=vmem)


def _tile(n, pref):
    best = None
    t = LANE
    while t <= min(n, pref):
        if n % t == 0:
            best = t
        t += LANE
    assert best is not None, (n, pref)
    return best


def _split3(x):
    x1 = x.astype(BF16)
    r1 = x - x1.astype(F32)
    x2 = r1.astype(BF16)
    r2 = r1 - x2.astype(F32)
    return x1, x2, r2.astype(BF16)


def _log_sigmoid(x):
    return -(jnp.maximum(-x, 0.0) + jnp.log1p(jnp.exp(-jnp.abs(x))))


def _sigmoid(x):
    return 1.0 / (1.0 + jnp.exp(-x))


def _rmsnorm_kernel(x_ref, g_ref, o_ref):
    x = x_ref[...]
    ms = jnp.mean(x * x, axis=-1, keepdims=True)
    o_ref[...] = (x * lax.rsqrt(ms + EPS) * g_ref[...]).astype(o_ref.dtype)


def rmsnorm(x, g, out_dtype):
    n, d = x.shape
    tm = _tile(n, 512)
    return pl.pallas_call(
        _rmsnorm_kernel,
        grid=(n // tm,),
        in_specs=[pl.BlockSpec((tm, d), lambda i: (i, 0)),
                  pl.BlockSpec((1, d), lambda i: (0, 0))],
        out_specs=pl.BlockSpec((tm, d), lambda i: (i, 0)),
        out_shape=jax.ShapeDtypeStruct((n, d), out_dtype),
        compiler_params=_params(("parallel",)),
        name="rmsnorm",
    )(x, g.reshape(1, d))


def _final_norm_kernel(x_ref, g_ref, o_ref):
    x = x_ref[0]
    ms = jnp.mean(x * x, axis=-1, keepdims=True)
    o_ref[0] = x * lax.rsqrt(ms + EPS) * g_ref[...]


def final_rmsnorm(h3, g):
    b, tp, d = h3.shape
    nblk = (tp - FRONT) // LANE
    return pl.pallas_call(
        _final_norm_kernel,
        grid=(b, nblk),
        in_specs=[pl.BlockSpec((1, LANE, d), lambda i, j: (i, j + FRONT // LANE, 0)),
                  pl.BlockSpec((1, d), lambda i, j: (0, 0))],
        out_specs=pl.BlockSpec((1, LANE, d), lambda i, j: (i, j, 0)),
        out_shape=jax.ShapeDtypeStruct((b, tp - FRONT, d), F32),
        compiler_params=_params(("parallel", "parallel")),
        name="final_norm",
    )(h3, g.reshape(1, d))


def _group_sum_matrix():
    r = lax.broadcasted_iota(jnp.int32, (LANE, LANE), 0) // FOX_DH
    c = lax.broadcasted_iota(jnp.int32, (LANE, LANE), 1) // FOX_DH
    return jnp.where(r == c, 1.0, 0.0).astype(BF16)


def _matmul_kernel(*refs, has_resid, n_norm_tiles, n_valid):
    x_ref, w_ref = refs[0], refs[1]
    k = 2
    gain_ref = resid_ref = None
    if n_norm_tiles:
        gain_ref = refs[k]; k += 1
    if has_resid:
        resid_ref = refs[k]; k += 1
    o_ref, wb_ref = refs[k], refs[k + 1]

    @pl.when(pl.program_id(1) == 0)
    def _():
        w = w_ref[0]
        if n_valid is not None:
            col = lax.broadcasted_iota(jnp.int32, (1, w.shape[1]), 1)
            w = jnp.where(col < n_valid, w, 0.0)
        wb_ref[...] = w.astype(BF16)

    acc = jnp.dot(x_ref[...], wb_ref[...], preferred_element_type=F32)
    if has_resid:
        acc = resid_ref[...] + acc

    if not n_norm_tiles:
        o_ref[...] = acc.astype(o_ref.dtype)
        return

    j = pl.program_id(0)

    @pl.when(j >= n_norm_tiles)
    def _():
        o_ref[...] = acc.astype(o_ref.dtype)

    @pl.when(j < n_norm_tiles)
    def _():
        gmat = _group_sum_matrix()
        tn = acc.shape[1]
        for c in range(tn // LANE):
            y = acc[:, c * LANE:(c + 1) * LANE]
            y2 = y * y
            hi = y2.astype(BF16)
            lo = (y2 - hi.astype(F32)).astype(BF16)
            ss = (jnp.dot(hi, gmat, preferred_element_type=F32)
                  + jnp.dot(lo, gmat, preferred_element_type=F32))
            yn = y * lax.rsqrt(ss * (1.0 / FOX_DH) + EPS) * gain_ref[:, c * LANE:(c + 1) * LANE]
            o_ref[:, c * LANE:(c + 1) * LANE] = yn.astype(o_ref.dtype)


def matmul(x, w, layer, n_cols, out_dtype, resid=None, head_gain=None, n_norm_cols=0, col_start=0, n_valid=None,
           tm_pref=512, tn_pref=1024):
    m, kdim = x.shape
    tm = _tile(m, tm_pref)
    tn = _tile(n_cols, tn_pref)
    n_norm_tiles = n_norm_cols // tn
    assert n_norm_tiles * tn == n_norm_cols and col_start % tn == 0
    assert n_valid is None or n_cols == tn
    col_off = col_start // tn
    in_specs = [pl.BlockSpec((tm, kdim), lambda j, i: (i, 0)),
                pl.BlockSpec((1, kdim, tn), lambda j, i: (layer, 0, j + col_off))]
    args = [x, w]
    if n_norm_tiles:
        in_specs.append(pl.BlockSpec((1, tn), lambda j, i: (0, j)))
        args.append(head_gain)
    if resid is not None:
        in_specs.append(pl.BlockSpec((tm, tn), lambda j, i: (i, j)))
        args.append(resid)
    return pl.pallas_call(
        functools.partial(_matmul_kernel, has_resid=resid is not None, n_norm_tiles=n_norm_tiles, n_valid=n_valid),
        grid=(n_cols // tn, m // tm),
        in_specs=in_specs,
        out_specs=pl.BlockSpec((tm, tn), lambda j, i: (i, j)),
        out_shape=jax.ShapeDtypeStruct((m, n_cols), out_dtype),
        scratch_shapes=[pltpu.VMEM((kdim, tn), BF16)],
        compiler_params=_params(("arbitrary", "arbitrary")),
        name="matmul",
    )(*args)


def _mlstm_kernel(bi_ref, bf_ref, q_ref, k_ref, v_ref, og_ref, ig_ref, fg_ref, gain_ref,
                  out_ref, c_ref, m_ref):
    ci = pl.program_id(2)

    @pl.when(ci == 0)
    def _():
        c_ref[...] = jnp.zeros_like(c_ref)
        m_ref[...] = jnp.zeros_like(m_ref)

    for g in range(ML_GROUP):
        qk_cols = slice(g * ML_DQK, (g + 1) * ML_DQK)
        v_cols = slice(g * ML_DV, (g + 1) * ML_DV)
        _mlstm_head_chunk(bi_ref[pl.program_id(1) * ML_GROUP + g], bf_ref[pl.program_id(1) * ML_GROUP + g],
                          q_ref[0, :, qk_cols], k_ref[0, :, qk_cols], v_ref[0, :, v_cols], og_ref[0, :, v_cols],
                          ig_ref[0, g], fg_ref[0, g], gain_ref[:, v_cols],
                          out_ref.at[0, :, v_cols], c_ref.at[g], m_ref.at[g], ci)


def _mlstm_head_chunk(b_i, b_f, q, k, v, og, ig, fg, gain, out_ref, c_ref, m_ref, ci):
    L = ML_CHUNK
    pos = ci * L + lax.broadcasted_iota(jnp.int32, (1, L), 1)
    dummy = pos < N_DUMMY
    ig = ig + b_i
    fg = fg + b_f
    li = jnp.where(dummy, NEG_BIG, GATE_CAP * jnp.tanh(ig / GATE_CAP))
    lf = jnp.where(dummy, 0.0, _log_sigmoid(GATE_CAP * jnp.tanh(fg / GATE_CAP)))

    t_i = lax.broadcasted_iota(jnp.int32, (L, L), 0)
    s_i = lax.broadcasted_iota(jnp.int32, (L, L), 1)
    tril = s_i <= t_i
    b_col = jnp.sum(jnp.where(tril, jnp.broadcast_to(lf, (L, L)), 0.0), axis=1, keepdims=True)
    b_t = jnp.broadcast_to(b_col, (L, L))
    b_s = b_t.T
    d_log = jnp.where(tril, b_t - b_s + li, -jnp.inf)
    m_prev = m_ref[0:1, 0:1]
    inter = b_col + m_prev
    m_t = jnp.maximum(inter, jnp.max(d_log, axis=1, keepdims=True))

    scale = ML_DQK ** -0.5
    s = lax.dot_general(q, k, (((1,), (1,)), ((), ())), preferred_element_type=F32) * scale
    p = (s * jnp.exp(d_log - m_t)).astype(BF16)
    w_inter = jnp.exp(inter - m_t)

    lane = lax.broadcasted_iota(jnp.int32, (L, LANE), 1)
    v_ext = jnp.concatenate([v, jnp.where(lane == 0, 1.0, 0.0).astype(BF16)], axis=1)
    c_prev = c_ref[...]
    num_ext = (jnp.dot(p, v_ext, preferred_element_type=F32)
               + w_inter * jnp.dot(q, c_prev.astype(BF16), preferred_element_type=F32))
    num = num_ext[:, :ML_DV]
    den = num_ext[:, ML_DV:ML_DV + 1]
    hh = num / jnp.maximum(jnp.abs(den), jnp.exp(-m_t))
    ms = jnp.mean(hh * hh, axis=1, keepdims=True)
    hn = hh * lax.rsqrt(ms + EPS) * gain
    out_ref[...] = (_sigmoid(og.astype(F32)) * hn).astype(out_ref.dtype)

    g_tot = jnp.sum(lf, axis=1, keepdims=True)
    a = g_tot - b_s[0:1, :] + li
    m_loc = jnp.max(a, axis=1, keepdims=True)
    w_loc = jnp.exp(a - m_loc)
    k_t = (k.astype(F32) * scale).T
    c_loc = jnp.dot((k_t * w_loc).astype(BF16), v_ext, preferred_element_type=F32)
    m_new = jnp.maximum(g_tot + m_prev, m_loc)
    sp = jnp.exp(g_tot + m_prev - m_new)
    sl = jnp.exp(m_loc - m_new)
    c_ref[...] = sp * c_prev + sl * c_loc
    m_ref[...] = jnp.broadcast_to(m_new, m_ref.shape)


def mlstm_core(proj, ig, fg, b_i, b_f, h_gain, batch, tp):
    L = ML_CHUNK
    proj3 = proj.reshape(batch, tp, proj.shape[1])
    G = ML_GROUP
    ng = ML_HEADS // G
    assert 2 * ML_HEADS * ML_DQK == ML_HEADS * ML_DV
    smem = pl.BlockSpec(memory_space=pltpu.SMEM)
    out = pl.pallas_call(
        _mlstm_kernel,
        grid=(batch, ng, tp // L),
        in_specs=[smem, smem,
                  pl.BlockSpec((1, L, G * ML_DQK), lambda b, h, c: (b, c, h)),
                  pl.BlockSpec((1, L, G * ML_DQK), lambda b, h, c: (b, c, ng + h)),
                  pl.BlockSpec((1, L, G * ML_DV), lambda b, h, c: (b, c, ng + h)),
                  pl.BlockSpec((1, L, G * ML_DV), lambda b, h, c: (b, c, 2 * ng + h)),
                  pl.BlockSpec((1, G, 1, L), lambda b, h, c: (b, h, 0, c)),
                  pl.BlockSpec((1, G, 1, L), lambda b, h, c: (b, h, 0, c)),
                  pl.BlockSpec((1, G * ML_DV), lambda b, h, c: (0, h))],
        out_specs=pl.BlockSpec((1, L, G * ML_DV), lambda b, h, c: (b, c, h)),
        out_shape=jax.ShapeDtypeStruct((batch, tp, ML_HEADS * ML_DV), BF16),
        scratch_shapes=[pltpu.VMEM((G, ML_DQK, ML_DV + LANE), F32), pltpu.VMEM((G, 8, LANE), F32)],
        compiler_params=_params(("parallel", "parallel", "arbitrary")),
        name="mlstm_core",
    )(b_i, b_f, proj3, proj3, proj3, proj3, ig, fg, h_gain.reshape(1, -1))
    return out.reshape(batch * tp, ML_HEADS * ML_DV)


def _fox_cumsum_kernel(fg_ref, bf_ref, o_ref, carry_ref):
    ci = pl.program_id(1)

    @pl.when(ci == 0)
    def _():
        carry_ref[...] = jnp.zeros_like(carry_ref)

    pos = ci * LANE + lax.broadcasted_iota(jnp.int32, (LANE, 1), 0)
    x = jnp.where(pos < N_DUMMY, 0.0, LOG2E * _log_sigmoid(fg_ref[0] + bf_ref[...]))
    r = lax.broadcasted_iota(jnp.int32, (LANE, LANE), 0)
    c = lax.broadcasted_iota(jnp.int32, (LANE, LANE), 1)
    lower = jnp.where(c <= r, 1.0, 0.0).astype(BF16)
    x1, x2, x3 = _split3(x)
    cs = (jnp.dot(lower, x1, preferred_element_type=F32)
          + jnp.dot(lower, x2, preferred_element_type=F32)
          + jnp.dot(lower, x3, preferred_element_type=F32)) + carry_ref[0:1, :]
    carry_ref[...] = jnp.broadcast_to(cs[LANE - 1:LANE, :], carry_ref.shape)
    for h in range(FOX_HEADS):
        o_ref[0, h] = jnp.broadcast_to(cs[:, h:h + 1], (LANE, LANE))


def fox_cumsum(fg, b_f, batch, tp):
    fg3 = fg.reshape(batch, tp, LANE)
    bias = jnp.zeros((1, LANE), F32).at[0, :FOX_HEADS].set(b_f)
    return pl.pallas_call(
        _fox_cumsum_kernel,
        grid=(batch, tp // LANE),
        in_specs=[pl.BlockSpec((1, LANE, LANE), lambda i, c: (i, c, 0)),
                  pl.BlockSpec((1, LANE), lambda i, c: (0, 0))],
        out_specs=pl.BlockSpec((1, FOX_HEADS, LANE, LANE), lambda i, c: (i, 0, c, 0)),
        out_shape=jax.ShapeDtypeStruct((batch, FOX_HEADS, tp, LANE), F32),
        scratch_shapes=[pltpu.VMEM((8, LANE), F32)],
        compiler_params=_params(("parallel", "arbitrary")),
        name="fox_cumsum",
    )(fg3, bias)


def _fox_attn_kernel(q_ref, k_ref, v_ref, og_ref, cb_ref, o_ref,
                     qm_ref, vt_ref, ka_ref, s_ref, m_ref, l_ref, acc_ref, *, blk, nblk):
    qi = pl.program_id(2)
    nt = (((1,), (1,)), ((), ()))
    sub = blk // LANE
    lane = lax.broadcasted_iota(jnp.int32, (1, LANE), 1)
    lo = lane < FOX_DH
    bias_lane = (FOX_DH, 0)

    @pl.when(qi == 0)
    def _():
        row = lax.broadcasted_iota(jnp.int32, (LANE, 1), 0)
        for j in range(nblk * sub):
            rows = slice(j * LANE, (j + 1) * LANE)
            dst = (j // sub, slice((j % sub) * LANE, (j % sub + 1) * LANE))
            vt_ref[dst[0], :, dst[1]] = v_ref[0, rows, :].astype(F32).T.astype(BF16)
            kt = k_ref[0, rows, :].astype(F32)
            for hh in range(2):
                cb = cb_ref[0, hh, rows, :]
                if j == 0:
                    cb = jnp.where(row < N_DUMMY, -NEG_BIG, cb)
                c1, c2, c3 = (t.astype(F32) for t in _split3(cb))
                b0 = bias_lane[hh]
                aug = jnp.where(lane == b0, c1, jnp.where(lane == b0 + 1, c2,
                                jnp.where(lane == b0 + 2, c3, 0.0)))
                ka = jnp.where(lo if hh == 0 else jnp.logical_not(lo), kt, aug)
                ka_ref[hh, dst[0], dst[1], :] = ka.astype(BF16)

    q = q_ref[0].astype(F32)
    qm_ref[0] = jnp.where(lo, q, jnp.where((lane >= bias_lane[0]) & (lane < bias_lane[0] + 3), -1.0, 0.0)).astype(BF16)
    qm_ref[1] = jnp.where(lo, jnp.where(lane < bias_lane[1] + 3, -1.0, 0.0), q).astype(BF16)
    m_ref[...] = jnp.full_like(m_ref, -jnp.inf)
    l_ref[...] = jnp.zeros_like(l_ref)
    acc_ref[...] = jnp.zeros_like(acc_ref)

    def scores(ki, slot):
        for hh in range(2):
            s_ref[slot, hh] = lax.dot_general(ka_ref[hh, ki], qm_ref[hh], nt, preferred_element_type=F32)

    def softmax_pv(ki, slot, diagonal):
        vt = vt_ref[ki]
        if diagonal:
            spos = lax.broadcasted_iota(jnp.int32, (blk, 1), 0)
            tpos = lax.broadcasted_iota(jnp.int32, (1, blk), 1)
            off = ki * blk
            valid = (spos <= tpos) & ((spos + off >= N_DUMMY) | (tpos + off < N_DUMMY))
        for hh in range(2):
            s = s_ref[slot, hh]
            if diagonal:
                s = jnp.where(valid, s, -jnp.inf)
            m_prev = m_ref[hh]
            m_new = jnp.maximum(m_prev, jnp.max(s, axis=0, keepdims=True))
            alpha = jnp.exp2(m_prev - m_new)
            p = jnp.exp2(s - m_new)
            l_ref[hh] = alpha * l_ref[hh] + jnp.sum(p, axis=0, keepdims=True)
            m_ref[hh] = m_new
            pv = jnp.dot(vt[hh * FOX_DH:(hh + 1) * FOX_DH, :], p.astype(BF16), preferred_element_type=F32)
            acc_ref[hh] = acc_ref[hh] * alpha + pv

    scores(0, 0)

    def body(i, carry):
        scores(2 * i + 1, 1)
        softmax_pv(2 * i, 0, False)
        scores(2 * i + 2, 0)
        softmax_pv(2 * i + 1, 1, False)
        return carry

    lax.fori_loop(0, qi // 2, body, 0)

    @pl.when(qi % 2 == 0)
    def _():
        softmax_pv(qi, 0, True)

    @pl.when(qi % 2 == 1)
    def _():
        scores(qi, 1)
        softmax_pv(qi - 1, 0, False)
        softmax_pv(qi, 1, True)

    y_t = jnp.concatenate([acc_ref[0] / l_ref[0], acc_ref[1] / l_ref[1]], axis=0)
    y = jnp.concatenate([y_t[:, j * LANE:(j + 1) * LANE].T for j in range(sub)], axis=0)
    o_ref[0] = (_sigmoid(og_ref[0].astype(F32)) * y).astype(o_ref.dtype)


def fox_attention(proj, cb, batch, tp):
    d = FOX_HEADS * FOX_DH
    blk = _tile(tp, 512)
    nblk = tp // blk
    proj3 = proj.reshape(batch, tp, proj.shape[1])
    out = pl.pallas_call(
        functools.partial(_fox_attn_kernel, blk=blk, nblk=nblk),
        grid=(batch, FOX_PAIRS, nblk),
        in_specs=[pl.BlockSpec((1, blk, LANE), lambda b, p, qi: (b, qi, p)),
                  pl.BlockSpec((1, tp, LANE), lambda b, p, qi: (b, 0, FOX_PAIRS + p)),
                  pl.BlockSpec((1, tp, LANE), lambda b, p, qi: (b, 0, 2 * FOX_PAIRS + p)),
                  pl.BlockSpec((1, blk, LANE), lambda b, p, qi: (b, qi, 3 * FOX_PAIRS + p)),
                  pl.BlockSpec((1, 2, tp, LANE), lambda b, p, qi: (b, p, 0, 0))],
        out_specs=pl.BlockSpec((1, blk, LANE), lambda b, p, qi: (b, qi, p)),
        out_shape=jax.ShapeDtypeStruct((batch, tp, d), BF16),
        scratch_shapes=[pltpu.VMEM((2, blk, LANE), BF16),
                        pltpu.VMEM((nblk, LANE, blk), BF16),
                        pltpu.VMEM((2, nblk, blk, LANE), BF16),
                        pltpu.VMEM((2, 2, blk, blk), F32),
                        pltpu.VMEM((2, 1, blk), F32), pltpu.VMEM((2, 1, blk), F32),
                        pltpu.VMEM((2, FOX_DH, blk), F32)],
        compiler_params=_params(("arbitrary", "arbitrary", "arbitrary")),
        name="fox_attention",
    )(proj3, proj3, proj3, proj3, cb)
    return out.reshape(batch * tp, d)


def _ffn_kernel(te_ref, nu_ref, x_ref, g_ref, wg_ref, wu_ref, wd_ref, *rest, has_resid, n_ff):
    if has_resid:
        resid_ref, o_ref, xb_ref, act_ref = rest
    else:
        (o_ref, xb_ref, act_ref), resid_ref = rest, None
    i = pl.program_id(0)
    f = pl.program_id(1)

    @pl.when(i < nu_ref[0])
    def _():
        def gate_up():
            xb = xb_ref[...]
            gate = jnp.dot(xb, wg_ref[0], preferred_element_type=F32)
            up = jnp.dot(xb, wu_ref[0], preferred_element_type=F32)
            return (gate * _sigmoid(gate) * up).astype(BF16)

        def down():
            return jnp.dot(act_ref[...], wd_ref[0], preferred_element_type=F32)

        @pl.when(f == 0)
        def _():
            x = x_ref[...]
            ms = jnp.mean(x * x, axis=-1, keepdims=True)
            xb_ref[...] = (x * lax.rsqrt(ms + EPS) * g_ref[...]).astype(BF16)
            o_ref[...] = resid_ref[...] if has_resid else jnp.zeros_like(o_ref)
            act_ref[...] = gate_up()

        @pl.when((f > 0) & (f < n_ff))
        def _():
            contrib = down()
            act = gate_up()
            o_ref[...] += contrib
            act_ref[...] = act

        @pl.when(f == n_ff)
        def _():
            o_ref[...] += down()

    @pl.when((i >= nu_ref[0]) & (f == 0))
    def _():
        o_ref[...] = jnp.zeros_like(o_ref)


def grouped_ffn(x, norm_g, w_gate, w_up, w_down, tile_expert, n_used, tm, resid=None):
    r, d = x.shape
    tf = FF_TILE
    n_ff = w_gate.shape[2] // tf
    n_tiles = r // tm
    assert n_tiles * tm == r and n_ff * tf == w_down.shape[1]

    def row_map(i, f, te, nu):
        return (jnp.minimum(i, nu[0] - 1), 0)

    def _ef(i, f, te, nu):
        ii = jnp.minimum(i, nu[0] - 1)
        return te[ii], jnp.where(i < nu[0], f, n_ff)

    def wcol_map(i, f, te, nu):
        e, fi = _ef(i, f, te, nu)
        return (e, 0, jnp.minimum(fi, n_ff - 1))

    def wrow_map(i, f, te, nu):
        e, fi = _ef(i, f, te, nu)
        return (e, jnp.maximum(fi - 1, 0), 0)

    in_specs = [pl.BlockSpec((tm, d), row_map),
                pl.BlockSpec((1, d), lambda i, f, te, nu: (0, 0)),
                pl.BlockSpec((1, d, tf), wcol_map),
                pl.BlockSpec((1, d, tf), wcol_map),
                pl.BlockSpec((1, tf, d), wrow_map)]
    args = [x, norm_g.reshape(1, d), w_gate, w_up, w_down]
    if resid is not None:
        in_specs.append(pl.BlockSpec((tm, d), row_map))
        args.append(resid)
    return pl.pallas_call(
        functools.partial(_ffn_kernel, has_resid=resid is not None, n_ff=n_ff),
        grid_spec=pltpu.PrefetchScalarGridSpec(
            num_scalar_prefetch=2,
            grid=(n_tiles, n_ff + 1),
            in_specs=in_specs,
            out_specs=pl.BlockSpec((tm, d), lambda i, f, te, nu: (i, 0)),
            scratch_shapes=[pltpu.VMEM((tm, d), BF16), pltpu.VMEM((tm, tf), BF16)]),
        out_shape=jax.ShapeDtypeStruct((r, d), F32),
        compiler_params=_params(("arbitrary", "arbitrary")),
        name="grouped_ffn",
    )(tile_expert, n_used, *args)


def _router_kernel(x_ref, g_ref, rt_ref, idx_ref, gate_ref, rank_ref, cnt_ref, base_ref, *, tm):
    i = pl.program_id(0)

    @pl.when(i == 0)
    def _():
        base_ref[...] = jnp.zeros_like(base_ref)

    x = x_ref[...]
    ms = jnp.mean(x * x, axis=-1, keepdims=True)
    hn = x * lax.rsqrt(ms + EPS) * g_ref[...]
    r1, r2, _ = _split3(rt_ref[...])
    h1, h2, _ = _split3(hn)
    nt = (((1,), (1,)), ((), ()))
    logits = (lax.dot_general(r1, h1, nt, preferred_element_type=F32)
              + lax.dot_general(r1, h2, nt, preferred_element_type=F32)
              + lax.dot_general(r2, h1, nt, preferred_element_type=F32))

    e_i = lax.broadcasted_iota(jnp.int32, (N_EXPERTS, tm), 0)
    v1 = jnp.max(logits, axis=0, keepdims=True)
    i1 = jnp.min(jnp.where(logits == v1, e_i, N_EXPERTS), axis=0, keepdims=True)
    sel1 = e_i == i1
    rest = jnp.where(sel1, -jnp.inf, logits)
    v2 = jnp.max(rest, axis=0, keepdims=True)
    i2 = jnp.min(jnp.where(rest == v2, e_i, N_EXPERTS), axis=0, keepdims=True)
    sel2 = e_i == i2
    e2 = jnp.exp(v2 - v1)
    den = 1.0 + e2
    idx_ref[...] = jnp.concatenate([i1, i2], axis=0)
    gate_ref[...] = jnp.concatenate([1.0 / den, e2 / den], axis=0)

    sel = jnp.where(sel1 | sel2, 1.0, 0.0)
    t_r = lax.broadcasted_iota(jnp.int32, (tm, tm), 0)
    t_c = lax.broadcasted_iota(jnp.int32, (tm, tm), 1)
    before = jnp.where(t_r < t_c, 1.0, 0.0).astype(BF16)
    tot = base_ref[:, 0:1] + jnp.dot(sel.astype(BF16), before, preferred_element_type=F32)
    rk1 = jnp.sum(jnp.where(sel1, tot, 0.0), axis=0, keepdims=True)
    rk2 = jnp.sum(jnp.where(sel2, tot, 0.0), axis=0, keepdims=True)
    rank_ref[...] = jnp.concatenate([rk1, rk2], axis=0).astype(jnp.int32)
    new_base = base_ref[...] + jnp.sum(sel, axis=1, keepdims=True)
    base_ref[...] = new_base
    cnt_ref[...] = new_base.astype(jnp.int32)


def router(h, norm_g, router_w):
    n, d = h.shape
    tm = _tile(n, 512)
    row2 = lambda i: (0, i)
    idx, gate, rank, cnt = pl.pallas_call(
        functools.partial(_router_kernel, tm=tm),
        grid=(n // tm,),
        in_specs=[pl.BlockSpec((tm, d), lambda i: (i, 0)),
                  pl.BlockSpec((1, d), lambda i: (0, 0)),
                  pl.BlockSpec((N_EXPERTS, d), lambda i: (0, 0))],
        out_specs=[pl.BlockSpec((TOP_K, tm), row2), pl.BlockSpec((TOP_K, tm), row2),
                   pl.BlockSpec((TOP_K, tm), row2), pl.BlockSpec((N_EXPERTS, LANE), lambda i: (0, 0))],
        out_shape=[jax.ShapeDtypeStruct((TOP_K, n), jnp.int32), jax.ShapeDtypeStruct((TOP_K, n), F32),
                   jax.ShapeDtypeStruct((TOP_K, n), jnp.int32), jax.ShapeDtypeStruct((N_EXPERTS, LANE), jnp.int32)],
        scratch_shapes=[pltpu.VMEM((N_EXPERTS, LANE), F32)],
        compiler_params=_params(("arbitrary",)),
        name="router",
    )(h, norm_g.reshape(1, d), router_w.T)
    return idx, gate, rank, cnt[:, 0]


def _row_copy(src_ref, s, dst_ref, t, sem):
    return pltpu.make_async_copy(src_ref.at[pl.ds(s, 1), :], dst_ref.at[pl.ds(t, 1), :], sem)


def _dispatch_kernel(pos_ref, h_ref, xs_in_ref, xs_ref, sem, *, tt):
    del xs_in_ref

    def start(j, carry):
        for kk in range(TOP_K):
            _row_copy(h_ref, j, xs_ref, pos_ref[0, kk, j], sem).start()
        return carry

    def wait(j, carry):
        for kk in range(TOP_K):
            _row_copy(h_ref, j, xs_ref, pos_ref[0, kk, j], sem).wait()
        return carry

    lax.fori_loop(0, tt, start, 0, unroll=8)
    lax.fori_loop(0, tt, wait, 0, unroll=8)


def dispatch(h, pos_tiles, n_rows, tt):
    n, d = h.shape
    xs0 = jnp.zeros((n_rows, d), h.dtype)
    any_spec = pl.BlockSpec(memory_space=pl.ANY)
    return pl.pallas_call(
        functools.partial(_dispatch_kernel, tt=tt),
        grid=(n // tt,),
        in_specs=[pl.BlockSpec((1, TOP_K, tt), lambda i: (i, 0, 0), memory_space=pltpu.SMEM),
                  pl.BlockSpec((tt, d), lambda i: (i, 0)), any_spec],
        out_specs=any_spec,
        out_shape=jax.ShapeDtypeStruct((n_rows, d), h.dtype),
        scratch_shapes=[pltpu.SemaphoreType.DMA],
        input_output_aliases={2: 0},
        compiler_params=_params(("arbitrary",)),
        name="moe_dispatch",
    )(pos_tiles, h, xs0)


def _combine_kernel(pos_ref, h_ref, gate_ref, ys_ref, o_ref, buf_ref, sem, *, tt):
    def start(j, carry):
        for kk in range(TOP_K):
            _row_copy(ys_ref, pos_ref[0, kk, j], buf_ref.at[kk], j, sem).start()
        return carry

    def wait(j, carry):
        for kk in range(TOP_K):
            _row_copy(ys_ref, pos_ref[0, kk, j], buf_ref.at[kk], j, sem).wait()
        return carry

    lax.fori_loop(0, tt, start, 0)
    lax.fori_loop(0, tt, wait, 0)
    g = gate_ref[...]
    o_ref[...] = h_ref[...] + (g[:, 0:1] * buf_ref[0] + g[:, 1:2] * buf_ref[1])


def combine(h, gate_cols, ys, pos_tiles, tt):
    n, d = h.shape
    return pl.pallas_call(
        functools.partial(_combine_kernel, tt=tt),
        grid=(n // tt,),
        in_specs=[pl.BlockSpec((1, TOP_K, tt), lambda i: (i, 0, 0), memory_space=pltpu.SMEM),
                  pl.BlockSpec((tt, d), lambda i: (i, 0)),
                  pl.BlockSpec((tt, TOP_K), lambda i: (i, 0)),
                  pl.BlockSpec(memory_space=pl.ANY)],
        out_specs=pl.BlockSpec((tt, d), lambda i: (i, 0)),
        out_shape=jax.ShapeDtypeStruct((n, d), F32),
        scratch_shapes=[pltpu.VMEM((TOP_K, tt, d), F32), pltpu.SemaphoreType.DMA],
        compiler_params=_params(("arbitrary",)),
        name="moe_combine",
    )(pos_tiles, h, gate_cols, ys)


def moe_layer(h, norm_g, router_w, w_gate, w_up, w_down, tm=512, tt=256):
    n, d = h.shape
    idx, gate, rank, counts = router(h, norm_g, router_w)
    tiles_e = (counts + tm - 1) // tm
    tile_end = jnp.cumsum(tiles_e)
    starts = (tile_end - tiles_e) * tm
    pos = rank
    for e in range(N_EXPERTS):
        pos = pos + jnp.where(idx == e, starts[e], 0)
    n_tiles = (TOP_K * n) // tm + N_EXPERTS
    tile_expert = jnp.minimum(jnp.searchsorted(tile_end, jnp.arange(n_tiles, dtype=jnp.int32), side="right"),
                              N_EXPERTS - 1).astype(jnp.int32)
    n_used = tile_end[-1:].astype(jnp.int32)
    tt = _tile(n, tt)
    pos_tiles = pos.reshape(TOP_K, n // tt, tt).transpose(1, 0, 2)
    xs = dispatch(h, pos_tiles, n_tiles * tm, tt)
    ys = grouped_ffn(xs, norm_g, w_gate, w_up, w_down, tile_expert, n_used, tm)
    return combine(h, gate.T, ys, pos_tiles, tt)


def dense_ffn(h, norm_g, w_gate, w_up, w_down, tm=512):
    n, d = h.shape
    tm = _tile(n, tm)
    n_tiles = n // tm
    return grouped_ffn(h, norm_g, w_gate, w_up, w_down,
                       jnp.zeros((n_tiles,), jnp.int32), jnp.full((1,), n_tiles, jnp.int32), tm, resid=h)


def mlstm_layer(h, batch, tp, j, norm_g, w_in, b_i, b_f, h_gain, w_out):
    d = h.shape[1]
    n_main = 2 * ML_HEADS * ML_DQK + 2 * ML_HEADS * ML_DV
    hn = rmsnorm(h, norm_g, BF16)
    proj = matmul(hn, w_in, j, n_main, BF16)
    gates = matmul(hn, w_in, j, LANE, F32, col_start=n_main, n_valid=2 * ML_HEADS)
    gates = gates[:, :2 * ML_HEADS].reshape(batch, tp, 2, ML_HEADS).transpose(2, 0, 3, 1)
    gates = gates.reshape(2, batch, ML_HEADS, 1, tp)
    y = mlstm_core(proj, gates[0], gates[1], b_i, b_f, h_gain, batch, tp)
    return matmul(y, w_out, j, d, F32, resid=h)


def fox_layer(h, batch, tp, j, norm_g, w_in, b_f, q_gain, k_gain, w_out):
    d = h.shape[1]
    n_main = 4 * d
    hn = rmsnorm(h, norm_g, BF16)
    head_gain = jnp.concatenate([jnp.tile(q_gain * (FOX_DH ** -0.5 * LOG2E), FOX_HEADS), jnp.tile(k_gain, FOX_HEADS),
                                 jnp.ones((2 * d,), F32)]).reshape(1, n_main)
    proj = matmul(hn, w_in, j, n_main, BF16, head_gain=head_gain, n_norm_cols=2 * d)
    fg = matmul(hn, w_in, j, LANE, F32, col_start=n_main, n_valid=FOX_HEADS)
    cb = fox_cumsum(fg, b_f, batch, tp)
    y = fox_attention(proj, cb, batch, tp)
    return matmul(y, w_out, j, d, F32, resid=h)


def kernel(x, meta_tokens, ml_norm, ml_w_in, ml_b_i, ml_b_f, ml_h_gain, ml_w_out, ffn_norm, ffn_w_gate, ffn_w_up, ffn_w_down, fox_norm, fox_w_in, fox_b_f, fox_q_gain, fox_k_gain, fox_w_out, moe_norm, moe_router, moe_w_gate, moe_w_up, moe_w_down, final_norm):
    batch, seq, d = x.shape
    tp = FRONT + seq
    depth = ml_norm.shape[0] + fox_norm.shape[0]
    front = jnp.concatenate([jnp.zeros((N_DUMMY, d), x.dtype), meta_tokens.astype(x.dtype)], axis=0)
    h = jnp.concatenate([jnp.broadcast_to(front[None], (batch, FRONT, d)), x], axis=1).reshape(batch * tp, d)
    bf = lambda w: w.astype(BF16)
    for i in range(depth):
        j = i // 2
        if i % 2 == 0:
            h = mlstm_layer(h, batch, tp, j, ml_norm[j], ml_w_in, ml_b_i[j], ml_b_f[j], ml_h_gain[j], ml_w_out)
            h = dense_ffn(h, ffn_norm[j], bf(ffn_w_gate[j:j + 1]), bf(ffn_w_up[j:j + 1]), bf(ffn_w_down[j:j + 1]))
        else:
            h = fox_layer(h, batch, tp, j, fox_norm[j], fox_w_in, fox_b_f[j], fox_q_gain[j], fox_k_gain[j], fox_w_out)
            h = moe_layer(h, moe_norm[j], moe_router[j], bf(moe_w_gate[j]), bf(moe_w_up[j]), bf(moe_w_down[j]))
    return final_rmsnorm(h.reshape(batch, tp, d), final_norm)
```

```python
import functools

import jax
import jax.numpy as jnp
from jax import lax
from jax.experimental import pallas as pl
from jax.experimental.pallas import tpu as pltpu

F32 = jnp.float32
BF16 = jnp.bfloat16

N_META = 16
EPS = 1e-6
LANE = 128
FRONT = 128
N_DUMMY = FRONT - N_META
ML_HEADS = 8
ML_DQK = 128
ML_DV = 256
ML_CHUNK = 128
ML_GROUP = 4
GATE_CAP = 15.0
NEG_BIG = -1e30
FOX_HEADS = 32
FOX_DH = 64
FOX_PAIRS = FOX_HEADS // 2
FOX_ROWS = 16
LOG2E = 1.4426950408889634
N_EXPERTS = 8
TOP_K = 2
FF_TILE = 512

VMEM_LIMIT = 56 * 1024 * 1024


def _params(sem, vmem=VMEM_LIMIT):
    return pltpu.CompilerParams(dimension_semantics=sem, vmem_limit_bytes=vmem)


def _tile(n, pref):
    best = None
    t = LANE
    while t <= min(n, pref):
        if n % t == 0:
            best = t
        t += LANE
    assert best is not None, (n, pref)
    return best


def _split3(x):
    x1 = x.astype(BF16)
    r1 = x - x1.astype(F32)
    x2 = r1.astype(BF16)
    r2 = r1 - x2.astype(F32)
    return x1, x2, r2.astype(BF16)


def _log_sigmoid(x):
    return -(jnp.maximum(-x, 0.0) + jnp.log1p(jnp.exp(-jnp.abs(x))))


def _sigmoid(x):
    return 1.0 / (1.0 + jnp.exp(-x))


def _rmsnorm_kernel(x_ref, g_ref, o_ref):
    x = x_ref[...]
    ms = jnp.mean(x * x, axis=-1, keepdims=True)
    o_ref[...] = (x * lax.rsqrt(ms + EPS) * g_ref[...]).astype(o_ref.dtype)


def rmsnorm(x, g, out_dtype):
    n, d = x.shape
    tm = _tile(n, 512)
    return pl.pallas_call(
        _rmsnorm_kernel,
        grid=(n // tm,),
        in_specs=[pl.BlockSpec((tm, d), lambda i: (i, 0)),
                  pl.BlockSpec((1, d), lambda i: (0, 0))],
        out_specs=pl.BlockSpec((tm, d), lambda i: (i, 0)),
        out_shape=jax.ShapeDtypeStruct((n, d), out_dtype),
        compiler_params=_params(("parallel",)),
        name="rmsnorm",
    )(x, g.reshape(1, d))


def _final_norm_kernel(x_ref, g_ref, o_ref):
    x = x_ref[0]
    ms = jnp.mean(x * x, axis=-1, keepdims=True)
    o_ref[0] = x * lax.rsqrt(ms + EPS) * g_ref[...]


def final_rmsnorm(h3, g):
    b, tp, d = h3.shape
    nblk = (tp - FRONT) // LANE
    return pl.pallas_call(
        _final_norm_kernel,
        grid=(b, nblk),
        in_specs=[pl.BlockSpec((1, LANE, d), lambda i, j: (i, j + FRONT // LANE, 0)),
                  pl.BlockSpec((1, d), lambda i, j: (0, 0))],
        out_specs=pl.BlockSpec((1, LANE, d), lambda i, j: (i, j, 0)),
        out_shape=jax.ShapeDtypeStruct((b, tp - FRONT, d), F32),
        compiler_params=_params(("parallel", "parallel")),
        name="final_norm",
    )(h3, g.reshape(1, d))


def _group_sum_matrix():
    r = lax.broadcasted_iota(jnp.int32, (LANE, LANE), 0) // FOX_DH
    c = lax.broadcasted_iota(jnp.int32, (LANE, LANE), 1) // FOX_DH
    return jnp.where(r == c, 1.0, 0.0).astype(BF16)


def _matmul_kernel(*refs, has_resid, n_norm_tiles, n_valid):
    x_ref, w_ref = refs[0], refs[1]
    k = 2
    gain_ref = resid_ref = None
    if n_norm_tiles:
        gain_ref = refs[k]; k += 1
    if has_resid:
        resid_ref = refs[k]; k += 1
    o_ref, wb_ref = refs[k], refs[k + 1]

    @pl.when(pl.program_id(1) == 0)
    def _():
        w = w_ref[0]
        if n_valid is not None:
            col = lax.broadcasted_iota(jnp.int32, (1, w.shape[1]), 1)
            w = jnp.where(col < n_valid, w, 0.0)
        wb_ref[...] = w.astype(BF16)

    acc = jnp.dot(x_ref[...], wb_ref[...], preferred_element_type=F32)
    if has_resid:
        acc = resid_ref[...] + acc

    if not n_norm_tiles:
        o_ref[...] = acc.astype(o_ref.dtype)
        return

    j = pl.program_id(0)

    @pl.when(j >= n_norm_tiles)
    def _():
        o_ref[...] = acc.astype(o_ref.dtype)

    @pl.when(j < n_norm_tiles)
    def _():
        gmat = _group_sum_matrix()
        tn = acc.shape[1]
        for c in range(tn // LANE):
            y = acc[:, c * LANE:(c + 1) * LANE]
            y2 = y * y
            hi = y2.astype(BF16)
            lo = (y2 - hi.astype(F32)).astype(BF16)
            ss = (jnp.dot(hi, gmat, preferred_element_type=F32)
                  + jnp.dot(lo, gmat, preferred_element_type=F32))
            yn = y * lax.rsqrt(ss * (1.0 / FOX_DH) + EPS) * gain_ref[:, c * LANE:(c + 1) * LANE]
            o_ref[:, c * LANE:(c + 1) * LANE] = yn.astype(o_ref.dtype)


def matmul(x, w, layer, n_cols, out_dtype, resid=None, head_gain=None, n_norm_cols=0, col_start=0, n_valid=None,
           tm_pref=512, tn_pref=1024):
    m, kdim = x.shape
    tm = _tile(m, tm_pref)
    tn = _tile(n_cols, tn_pref)
    n_norm_tiles = n_norm_cols // tn
    assert n_norm_tiles * tn == n_norm_cols and col_start % tn == 0
    assert n_valid is None or n_cols == tn
    col_off = col_start // tn
    in_specs = [pl.BlockSpec((tm, kdim), lambda j, i: (i, 0)),
                pl.BlockSpec((1, kdim, tn), lambda j, i: (layer, 0, j + col_off))]
    args = [x, w]
    if n_norm_tiles:
        in_specs.append(pl.BlockSpec((1, tn), lambda j, i: (0, j)))
        args.append(head_gain)
    if resid is not None:
        in_specs.append(pl.BlockSpec((tm, tn), lambda j, i: (i, j)))
        args.append(resid)
    return pl.pallas_call(
        functools.partial(_matmul_kernel, has_resid=resid is not None, n_norm_tiles=n_norm_tiles, n_valid=n_valid),
        grid=(n_cols // tn, m // tm),
        in_specs=in_specs,
        out_specs=pl.BlockSpec((tm, tn), lambda j, i: (i, j)),
        out_shape=jax.ShapeDtypeStruct((m, n_cols), out_dtype),
        scratch_shapes=[pltpu.VMEM((kdim, tn), BF16)],
        compiler_params=_params(("arbitrary", "arbitrary")),
        name="matmul",
    )(*args)


def _mlstm_kernel(bi_ref, bf_ref, q_ref, k_ref, v_ref, og_ref, ig_ref, fg_ref, gain_ref,
                  out_ref, c_ref, m_ref):
    ci = pl.program_id(2)

    @pl.when(ci == 0)
    def _():
        c_ref[...] = jnp.zeros_like(c_ref)
        m_ref[...] = jnp.zeros_like(m_ref)

    for g in range(ML_GROUP):
        qk_cols = slice(g * ML_DQK, (g + 1) * ML_DQK)
        v_cols = slice(g * ML_DV, (g + 1) * ML_DV)
        _mlstm_head_chunk(bi_ref[pl.program_id(1) * ML_GROUP + g], bf_ref[pl.program_id(1) * ML_GROUP + g],
                          q_ref[0, :, qk_cols], k_ref[0, :, qk_cols], v_ref[0, :, v_cols], og_ref[0, :, v_cols],
                          ig_ref[0, g], fg_ref[0, g], gain_ref[:, v_cols],
                          out_ref.at[0, :, v_cols], c_ref.at[g], m_ref.at[g], ci)


def _mlstm_head_chunk(b_i, b_f, q, k, v, og, ig, fg, gain, out_ref, c_ref, m_ref, ci):
    L = ML_CHUNK
    pos = ci * L + lax.broadcasted_iota(jnp.int32, (1, L), 1)
    dummy = pos < N_DUMMY
    ig = ig + b_i
    fg = fg + b_f
    li = jnp.where(dummy, NEG_BIG, GATE_CAP * jnp.tanh(ig / GATE_CAP))
    lf = jnp.where(dummy, 0.0, _log_sigmoid(GATE_CAP * jnp.tanh(fg / GATE_CAP)))

    t_i = lax.broadcasted_iota(jnp.int32, (L, L), 0)
    s_i = lax.broadcasted_iota(jnp.int32, (L, L), 1)
    tril = s_i <= t_i
    b_col = jnp.sum(jnp.where(tril, jnp.broadcast_to(lf, (L, L)), 0.0), axis=1, keepdims=True)
    b_t = jnp.broadcast_to(b_col, (L, L))
    b_s = b_t.T
    d_log = jnp.where(tril, b_t - b_s + li, -jnp.inf)
    m_prev = m_ref[0:1, 0:1]
    inter = b_col + m_prev
    m_t = jnp.maximum(inter, jnp.max(d_log, axis=1, keepdims=True))

    scale = ML_DQK ** -0.5
    s = lax.dot_general(q, k, (((1,), (1,)), ((), ())), preferred_element_type=F32) * scale
    p = (s * jnp.exp(d_log - m_t)).astype(BF16)
    w_inter = jnp.exp(inter - m_t)

    lane = lax.broadcasted_iota(jnp.int32, (L, LANE), 1)
    v_ext = jnp.concatenate([v, jnp.where(lane == 0, 1.0, 0.0).astype(BF16)], axis=1)
    c_prev = c_ref[...]
    num_ext = (jnp.dot(p, v_ext, preferred_element_type=F32)
               + w_inter * jnp.dot(q, c_prev.astype(BF16), preferred_element_type=F32))
    num = num_ext[:, :ML_DV]
    den = num_ext[:, ML_DV:ML_DV + 1]
    hh = num / jnp.maximum(jnp.abs(den), jnp.exp(-m_t))
    ms = jnp.mean(hh * hh, axis=1, keepdims=True)
    hn = hh * lax.rsqrt(ms + EPS) * gain
    out_ref[...] = (_sigmoid(og.astype(F32)) * hn).astype(out_ref.dtype)

    g_tot = jnp.sum(lf, axis=1, keepdims=True)
    a = g_tot - b_s[0:1, :] + li
    m_loc = jnp.max(a, axis=1, keepdims=True)
    w_loc = jnp.exp(a - m_loc)
    k_t = (k.astype(F32) * scale).T
    c_loc = jnp.dot((k_t * w_loc).astype(BF16), v_ext, preferred_element_type=F32)
    m_new = jnp.maximum(g_tot + m_prev, m_loc)
    sp = jnp.exp(g_tot + m_prev - m_new)
    sl = jnp.exp(m_loc - m_new)
    c_ref[...] = sp * c_prev + sl * c_loc
    m_ref[...] = jnp.broadcast_to(m_new, m_ref.shape)


def mlstm_core(proj, ig, fg, b_i, b_f, h_gain, batch, tp):
    L = ML_CHUNK
    proj3 = proj.reshape(batch, tp, proj.shape[1])
    G = ML_GROUP
    ng = ML_HEADS // G
    assert 2 * ML_HEADS * ML_DQK == ML_HEADS * ML_DV
    smem = pl.BlockSpec(memory_space=pltpu.SMEM)
    out = pl.pallas_call(
        _mlstm_kernel,
        grid=(batch, ng, tp // L),
        in_specs=[smem, smem,
                  pl.BlockSpec((1, L, G * ML_DQK), lambda b, h, c: (b, c, h)),
                  pl.BlockSpec((1, L, G * ML_DQK), lambda b, h, c: (b, c, ng + h)),
                  pl.BlockSpec((1, L, G * ML_DV), lambda b, h, c: (b, c, ng + h)),
                  pl.BlockSpec((1, L, G * ML_DV), lambda b, h, c: (b, c, 2 * ng + h)),
                  pl.BlockSpec((1, G, 1, L), lambda b, h, c: (b, h, 0, c)),
                  pl.BlockSpec((1, G, 1, L), lambda b, h, c: (b, h, 0, c)),
                  pl.BlockSpec((1, G * ML_DV), lambda b, h, c: (0, h))],
        out_specs=pl.BlockSpec((1, L, G * ML_DV), lambda b, h, c: (b, c, h)),
        out_shape=jax.ShapeDtypeStruct((batch, tp, ML_HEADS * ML_DV), BF16),
        scratch_shapes=[pltpu.VMEM((G, ML_DQK, ML_DV + LANE), F32), pltpu.VMEM((G, 8, LANE), F32)],
        compiler_params=_params(("parallel", "parallel", "arbitrary")),
        name="mlstm_core",
    )(b_i, b_f, proj3, proj3, proj3, proj3, ig, fg, h_gain.reshape(1, -1))
    return out.reshape(batch * tp, ML_HEADS * ML_DV)


def _fox_cumsum_kernel(fg_ref, bf_ref, o_ref, carry_ref):
    ci = pl.program_id(1)

    @pl.when(ci == 0)
    def _():
        carry_ref[...] = jnp.zeros_like(carry_ref)

    pos = ci * LANE + lax.broadcasted_iota(jnp.int32, (LANE, 1), 0)
    x = jnp.where(pos < N_DUMMY, 0.0, LOG2E * _log_sigmoid(fg_ref[0] + bf_ref[...]))
    r = lax.broadcasted_iota(jnp.int32, (LANE, LANE), 0)
    c = lax.broadcasted_iota(jnp.int32, (LANE, LANE), 1)
    lower = jnp.where(c <= r, 1.0, 0.0).astype(BF16)
    x1, x2, x3 = _split3(x)
    cs = (jnp.dot(lower, x1, preferred_element_type=F32)
          + jnp.dot(lower, x2, preferred_element_type=F32)
          + jnp.dot(lower, x3, preferred_element_type=F32)) + carry_ref[0:1, :]
    carry_ref[...] = jnp.broadcast_to(cs[LANE - 1:LANE, :], carry_ref.shape)
    for h in range(FOX_HEADS):
        o_ref[0, h] = jnp.broadcast_to(cs[:, h:h + 1], (LANE, LANE))


def fox_cumsum(fg, b_f, batch, tp):
    fg3 = fg.reshape(batch, tp, LANE)
    bias = jnp.zeros((1, LANE), F32).at[0, :FOX_HEADS].set(b_f)
    return pl.pallas_call(
        _fox_cumsum_kernel,
        grid=(batch, tp // LANE),
        in_specs=[pl.BlockSpec((1, LANE, LANE), lambda i, c: (i, c, 0)),
                  pl.BlockSpec((1, LANE), lambda i, c: (0, 0))],
        out_specs=pl.BlockSpec((1, FOX_HEADS, LANE, LANE), lambda i, c: (i, 0, c, 0)),
        out_shape=jax.ShapeDtypeStruct((batch, FOX_HEADS, tp, LANE), F32),
        scratch_shapes=[pltpu.VMEM((8, LANE), F32)],
        compiler_params=_params(("parallel", "arbitrary")),
        name="fox_cumsum",
    )(fg3, bias)


def _fox_attn_kernel(q_ref, k_ref, v_ref, og_ref, cb_ref, o_ref,
                     qm_ref, vt_ref, ka_ref, s_ref, m_ref, l_ref, acc_ref, *, blk, nblk):
    qi = pl.program_id(2)
    nt = (((1,), (1,)), ((), ()))
    sub = blk // LANE
    lane = lax.broadcasted_iota(jnp.int32, (1, LANE), 1)
    lo = lane < FOX_DH
    bias_lane = (FOX_DH, 0)

    @pl.when(qi == 0)
    def _():
        row = lax.broadcasted_iota(jnp.int32, (LANE, 1), 0)
        for j in range(nblk * sub):
            rows = slice(j * LANE, (j + 1) * LANE)
            dst = (j // sub, slice((j % sub) * LANE, (j % sub + 1) * LANE))
            vt_ref[dst[0], :, dst[1]] = v_ref[0, rows, :].astype(F32).T.astype(BF16)
            kt = k_ref[0, rows, :].astype(F32)
            for hh in range(2):
                cb = cb_ref[0, hh, rows, :]
                if j == 0:
                    cb = jnp.where(row < N_DUMMY, -NEG_BIG, cb)
                c1, c2, c3 = (t.astype(F32) for t in _split3(cb))
                b0 = bias_lane[hh]
                aug = jnp.where(lane == b0, c1, jnp.where(lane == b0 + 1, c2,
                                jnp.where(lane == b0 + 2, c3, 0.0)))
                ka = jnp.where(lo if hh == 0 else jnp.logical_not(lo), kt, aug)
                ka_ref[hh, dst[0], dst[1], :] = ka.astype(BF16)

    q = q_ref[0].astype(F32)
    qm_ref[0] = jnp.where(lo, q, jnp.where((lane >= bias_lane[0]) & (lane < bias_lane[0] + 3), -1.0, 0.0)).astype(BF16)
    qm_ref[1] = jnp.where(lo, jnp.where(lane < bias_lane[1] + 3, -1.0, 0.0), q).astype(BF16)
    m_ref[...] = jnp.full_like(m_ref, -jnp.inf)
    l_ref[...] = jnp.zeros_like(l_ref)
    acc_ref[...] = jnp.zeros_like(acc_ref)

    def scores(ki, slot):
        for hh in range(2):
            s_ref[slot, hh] = lax.dot_general(ka_ref[hh, ki], qm_ref[hh], nt, preferred_element_type=F32)

    def softmax_pv(ki, slot, diagonal):
        vt = vt_ref[ki]
        if diagonal:
            spos = lax.broadcasted_iota(jnp.int32, (blk, 1), 0)
            tpos = lax.broadcasted_iota(jnp.int32, (1, blk), 1)
            off = ki * blk
            valid = (spos <= tpos) & ((spos + off >= N_DUMMY) | (tpos + off < N_DUMMY))
        for hh in range(2):
            s = s_ref[slot, hh]
            if diagonal:
                s = jnp.where(valid, s, -jnp.inf)
            m_prev = m_ref[hh]
            m_new = jnp.maximum(m_prev, jnp.max(s, axis=0, keepdims=True))
            alpha = jnp.exp2(m_prev - m_new)
            p = jnp.exp2(s - m_new)
            l_ref[hh] = alpha * l_ref[hh] + jnp.sum(p, axis=0, keepdims=True)
            m_ref[hh] = m_new
            pv = jnp.dot(vt[hh * FOX_DH:(hh + 1) * FOX_DH, :], p.astype(BF16), preferred_element_type=F32)
            acc_ref[hh] = acc_ref[hh] * alpha + pv

    scores(0, 0)

    def body(i, carry):
        scores(2 * i + 1, 1)
        softmax_pv(2 * i, 0, False)
        scores(2 * i + 2, 0)
        softmax_pv(2 * i + 1, 1, False)
        return carry

    lax.fori_loop(0, qi // 2, body, 0)

    @pl.when(qi % 2 == 0)
    def _():
        softmax_pv(qi, 0, True)

    @pl.when(qi % 2 == 1)
    def _():
        scores(qi, 1)
        softmax_pv(qi - 1, 0, False)
        softmax_pv(qi, 1, True)

    y_t = jnp.concatenate([acc_ref[0] / l_ref[0], acc_ref[1] / l_ref[1]], axis=0)
    y = jnp.concatenate([y_t[:, j * LANE:(j + 1) * LANE].T for j in range(sub)], axis=0)
    o_ref[0] = (_sigmoid(og_ref[0].astype(F32)) * y).astype(o_ref.dtype)


def fox_attention(proj, cb, batch, tp):
    d = FOX_HEADS * FOX_DH
    blk = _tile(tp, 512)
    nblk = tp // blk
    proj3 = proj.reshape(batch, tp, proj.shape[1])
    out = pl.pallas_call(
        functools.partial(_fox_attn_kernel, blk=blk, nblk=nblk),
        grid=(batch, FOX_PAIRS, nblk),
        in_specs=[pl.BlockSpec((1, blk, LANE), lambda b, p, qi: (b, qi, p)),
                  pl.BlockSpec((1, tp, LANE), lambda b, p, qi: (b, 0, FOX_PAIRS + p)),
                  pl.BlockSpec((1, tp, LANE), lambda b, p, qi: (b, 0, 2 * FOX_PAIRS + p)),
                  pl.BlockSpec((1, blk, LANE), lambda b, p, qi: (b, qi, 3 * FOX_PAIRS + p)),
                  pl.BlockSpec((1, 2, tp, LANE), lambda b, p, qi: (b, p, 0, 0))],
        out_specs=pl.BlockSpec((1, blk, LANE), lambda b, p, qi: (b, qi, p)),
        out_shape=jax.ShapeDtypeStruct((batch, tp, d), BF16),
        scratch_shapes=[pltpu.VMEM((2, blk, LANE), BF16),
                        pltpu.VMEM((nblk, LANE, blk), BF16),
                        pltpu.VMEM((2, nblk, blk, LANE), BF16),
                        pltpu.VMEM((2, 2, blk, blk), F32),
                        pltpu.VMEM((2, 1, blk), F32), pltpu.VMEM((2, 1, blk), F32),
                        pltpu.VMEM((2, FOX_DH, blk), F32)],
        compiler_params=_params(("arbitrary", "arbitrary", "arbitrary")),
        name="fox_attention",
    )(proj3, proj3, proj3, proj3, cb)
    return out.reshape(batch * tp, d)


def _ffn_kernel(te_ref, nu_ref, x_ref, g_ref, wg_ref, wu_ref, wd_ref, *rest, has_resid, n_ff):
    if has_resid:
        resid_ref, o_ref, xb_ref, act_ref = rest
    else:
        (o_ref, xb_ref, act_ref), resid_ref = rest, None
    i = pl.program_id(0)
    f = pl.program_id(1)

    @pl.when(i < nu_ref[0])
    def _():
        def gate_up():
            xb = xb_ref[...]
            gate = jnp.dot(xb, wg_ref[0], preferred_element_type=F32)
            up = jnp.dot(xb, wu_ref[0], preferred_element_type=F32)
            return (gate * _sigmoid(gate) * up).astype(BF16)

        def down():
            return jnp.dot(act_ref[...], wd_ref[0], preferred_element_type=F32)

        @pl.when(f == 0)
        def _():
            x = x_ref[...]
            ms = jnp.mean(x * x, axis=-1, keepdims=True)
            xb_ref[...] = (x * lax.rsqrt(ms + EPS) * g_ref[...]).astype(BF16)
            o_ref[...] = resid_ref[...] if has_resid else jnp.zeros_like(o_ref)
            act_ref[...] = gate_up()

        @pl.when((f > 0) & (f < n_ff))
        def _():
            contrib = down()
            act = gate_up()
            o_ref[...] += contrib
            act_ref[...] = act

        @pl.when(f == n_ff)
        def _():
            o_ref[...] += down()

    @pl.when((i >= nu_ref[0]) & (f == 0))
    def _():
        o_ref[...] = jnp.zeros_like(o_ref)


def grouped_ffn(x, norm_g, w_gate, w_up, w_down, tile_expert, n_used, tm, resid=None):
    r, d = x.shape
    tf = FF_TILE
    n_ff = w_gate.shape[2] // tf
    n_tiles = r // tm
    assert n_tiles * tm == r and n_ff * tf == w_down.shape[1]

    def row_map(i, f, te, nu):
        return (jnp.minimum(i, nu[0] - 1), 0)

    def _ef(i, f, te, nu):
        ii = jnp.minimum(i, nu[0] - 1)
        return te[ii], jnp.where(i < nu[0], f, n_ff)

    def wcol_map(i, f, te, nu):
        e, fi = _ef(i, f, te, nu)
        return (e, 0, jnp.minimum(fi, n_ff - 1))

    def wrow_map(i, f, te, nu):
        e, fi = _ef(i, f, te, nu)
        return (e, jnp.maximum(fi - 1, 0), 0)

    in_specs = [pl.BlockSpec((tm, d), row_map),
                pl.BlockSpec((1, d), lambda i, f, te, nu: (0, 0)),
                pl.BlockSpec((1, d, tf), wcol_map),
                pl.BlockSpec((1, d, tf), wcol_map),
                pl.BlockSpec((1, tf, d), wrow_map)]
    args = [x, norm_g.reshape(1, d), w_gate, w_up, w_down]
    if resid is not None:
        in_specs.append(pl.BlockSpec((tm, d), row_map))
        args.append(resid)
    return pl.pallas_call(
        functools.partial(_ffn_kernel, has_resid=resid is not None, n_ff=n_ff),
        grid_spec=pltpu.PrefetchScalarGridSpec(
            num_scalar_prefetch=2,
            grid=(n_tiles, n_ff + 1),
            in_specs=in_specs,
            out_specs=pl.BlockSpec((tm, d), lambda i, f, te, nu: (i, 0)),
            scratch_shapes=[pltpu.VMEM((tm, d), BF16), pltpu.VMEM((tm, tf), BF16)]),
        out_shape=jax.ShapeDtypeStruct((r, d), F32),
        compiler_params=_params(("arbitrary", "arbitrary")),
        name="grouped_ffn",
    )(tile_expert, n_used, *args)


def _router_kernel(x_ref, g_ref, rt_ref, idx_ref, gate_ref, rank_ref, cnt_ref, base_ref, *, tm):
    i = pl.program_id(0)

    @pl.when(i == 0)
    def _():
        base_ref[...] = jnp.zeros_like(base_ref)

    x = x_ref[...]
    ms = jnp.mean(x * x, axis=-1, keepdims=True)
    hn = x * lax.rsqrt(ms + EPS) * g_ref[...]
    r1, r2, _ = _split3(rt_ref[...])
    h1, h2, _ = _split3(hn)
    nt = (((1,), (1,)), ((), ()))
    logits = (lax.dot_general(r1, h1, nt, preferred_element_type=F32)
              + lax.dot_general(r1, h2, nt, preferred_element_type=F32)
              + lax.dot_general(r2, h1, nt, preferred_element_type=F32))

    e_i = lax.broadcasted_iota(jnp.int32, (N_EXPERTS, tm), 0)
    v1 = jnp.max(logits, axis=0, keepdims=True)
    i1 = jnp.min(jnp.where(logits == v1, e_i, N_EXPERTS), axis=0, keepdims=True)
    sel1 = e_i == i1
    rest = jnp.where(sel1, -jnp.inf, logits)
    v2 = jnp.max(rest, axis=0, keepdims=True)
    i2 = jnp.min(jnp.where(rest == v2, e_i, N_EXPERTS), axis=0, keepdims=True)
    sel2 = e_i == i2
    e2 = jnp.exp(v2 - v1)
    den = 1.0 + e2
    idx_ref[...] = jnp.concatenate([i1, i2], axis=0)
    gate_ref[...] = jnp.concatenate([1.0 / den, e2 / den], axis=0)

    sel = jnp.where(sel1 | sel2, 1.0, 0.0)
    t_r = lax.broadcasted_iota(jnp.int32, (tm, tm), 0)
    t_c = lax.broadcasted_iota(jnp.int32, (tm, tm), 1)
    before = jnp.where(t_r < t_c, 1.0, 0.0).astype(BF16)
    tot = base_ref[:, 0:1] + jnp.dot(sel.astype(BF16), before, preferred_element_type=F32)
    rk1 = jnp.sum(jnp.where(sel1, tot, 0.0), axis=0, keepdims=True)
    rk2 = jnp.sum(jnp.where(sel2, tot, 0.0), axis=0, keepdims=True)
    rank_ref[...] = jnp.concatenate([rk1, rk2], axis=0).astype(jnp.int32)
    new_base = base_ref[...] + jnp.sum(sel, axis=1, keepdims=True)
    base_ref[...] = new_base
    cnt_ref[...] = new_base.astype(jnp.int32)


def router(h, norm_g, router_w):
    n, d = h.shape
    tm = _tile(n, 512)
    row2 = lambda i: (0, i)
    idx, gate, rank, cnt = pl.pallas_call(
        functools.partial(_router_kernel, tm=tm),
        grid=(n // tm,),
        in_specs=[pl.BlockSpec((tm, d), lambda i: (i, 0)),
                  pl.BlockSpec((1, d), lambda i: (0, 0)),
                  pl.BlockSpec((N_EXPERTS, d), lambda i: (0, 0))],
        out_specs=[pl.BlockSpec((TOP_K, tm), row2), pl.BlockSpec((TOP_K, tm), row2),
                   pl.BlockSpec((TOP_K, tm), row2), pl.BlockSpec((N_EXPERTS, LANE), lambda i: (0, 0))],
        out_shape=[jax.ShapeDtypeStruct((TOP_K, n), jnp.int32), jax.ShapeDtypeStruct((TOP_K, n), F32),
                   jax.ShapeDtypeStruct((TOP_K, n), jnp.int32), jax.ShapeDtypeStruct((N_EXPERTS, LANE), jnp.int32)],
        scratch_shapes=[pltpu.VMEM((N_EXPERTS, LANE), F32)],
        compiler_params=_params(("arbitrary",)),
        name="router",
    )(h, norm_g.reshape(1, d), router_w.T)
    return idx, gate, rank, cnt[:, 0]


def _row_copy(src_ref, s, dst_ref, t, sem):
    return pltpu.make_async_copy(src_ref.at[pl.ds(s, 1), :], dst_ref.at[pl.ds(t, 1), :], sem)


def _dispatch_kernel(pos_ref, h_ref, xs_in_ref, xs_ref, sem, *, tt):
    del xs_in_ref

    def start(j, carry):
        for kk in range(TOP_K):
            _row_copy(h_ref, j, xs_ref, pos_ref[0, kk, j], sem).start()
        return carry

    def wait(j, carry):
        for kk in range(TOP_K):
            _row_copy(h_ref, j, xs_ref, pos_ref[0, kk, j], sem).wait()
        return carry

    lax.fori_loop(0, tt, start, 0, unroll=8)
    lax.fori_loop(0, tt, wait, 0, unroll=8)


def dispatch(h, pos_tiles, n_rows, tt):
    n, d = h.shape
    xs0 = jnp.zeros((n_rows, d), h.dtype)
    any_spec = pl.BlockSpec(memory_space=pl.ANY)
    return pl.pallas_call(
        functools.partial(_dispatch_kernel, tt=tt),
        grid=(n // tt,),
        in_specs=[pl.BlockSpec((1, TOP_K, tt), lambda i: (i, 0, 0), memory_space=pltpu.SMEM),
                  pl.BlockSpec((tt, d), lambda i: (i, 0)), any_spec],
        out_specs=any_spec,
        out_shape=jax.ShapeDtypeStruct((n_rows, d), h.dtype),
        scratch_shapes=[pltpu.SemaphoreType.DMA],
        input_output_aliases={2: 0},
        compiler_params=_params(("arbitrary",)),
        name="moe_dispatch",
    )(pos_tiles, h, xs0)


def _combine_kernel(pos_ref, h_ref, gate_ref, ys_ref, o_ref, buf_ref, sem, *, tt):
    def start(j, carry):
        for kk in range(TOP_K):
            _row_copy(ys_ref, pos_ref[0, kk, j], buf_ref.at[kk], j, sem).start()
        return carry

    def wait(j, carry):
        for kk in range(TOP_K):
            _row_copy(ys_ref, pos_ref[0, kk, j], buf_ref.at[kk], j, sem).wait()
        return carry

    lax.fori_loop(0, tt, start, 0)
    lax.fori_loop(0, tt, wait, 0)
    g = gate_ref[...]
    o_ref[...] = h_ref[...] + (g[:, 0:1] * buf_ref[0] + g[:, 1:2] * buf_ref[1])


def combine(h, gate_cols, ys, pos_tiles, tt):
    n, d = h.shape
    return pl.pallas_call(
        functools.partial(_combine_kernel, tt=tt),
        grid=(n // tt,),
        in_specs=[pl.BlockSpec((1, TOP_K, tt), lambda i: (i, 0, 0), memory_space=pltpu.SMEM),
                  pl.BlockSpec((tt, d), lambda i: (i, 0)),
                  pl.BlockSpec((tt, TOP_K), lambda i: (i, 0)),
                  pl.BlockSpec(memory_space=pl.ANY)],
        out_specs=pl.BlockSpec((tt, d), lambda i: (i, 0)),
        out_shape=jax.ShapeDtypeStruct((n, d), F32),
        scratch_shapes=[pltpu.VMEM((TOP_K, tt, d), F32), pltpu.SemaphoreType.DMA],
        compiler_params=_params(("arbitrary",)),
        name="moe_combine",
    )(pos_tiles, h, gate_cols, ys)


def moe_layer(h, norm_g, router_w, w_gate, w_up, w_down, tm=512, tt=256):
    n, d = h.shape
    idx, gate, rank, counts = router(h, norm_g, router_w)
    tiles_e = (counts + tm - 1) // tm
    tile_end = jnp.cumsum(tiles_e)
    starts = (tile_end - tiles_e) * tm
    pos = rank
    for e in range(N_EXPERTS):
        pos = pos + jnp.where(idx == e, starts[e], 0)
    n_tiles = (TOP_K * n) // tm + N_EXPERTS
    tile_expert = jnp.minimum(jnp.searchsorted(tile_end, jnp.arange(n_tiles, dtype=jnp.int32), side="right"),
                              N_EXPERTS - 1).astype(jnp.int32)
    n_used = tile_end[-1:].astype(jnp.int32)
    tt = _tile(n, tt)
    pos_tiles = pos.reshape(TOP_K, n // tt, tt).transpose(1, 0, 2)
    xs = dispatch(h, pos_tiles, n_tiles * tm, tt)
    ys = grouped_ffn(xs, norm_g, w_gate, w_up, w_down, tile_expert, n_used, tm)
    return combine(h, gate.T, ys, pos_tiles, tt)


def dense_ffn(h, norm_g, w_gate, w_up, w_down, layer, tm=512):
    n, d = h.shape
    tm = _tile(n, tm)
    n_tiles = n // tm
    return grouped_ffn(h, norm_g, w_gate, w_up, w_down,
                       jnp.full((n_tiles,), layer, jnp.int32), jnp.full((1,), n_tiles, jnp.int32), tm, resid=h)


def mlstm_layer(h, batch, tp, j, norm_g, w_in, b_i, b_f, h_gain, w_out):
    d = h.shape[1]
    n_main = 2 * ML_HEADS * ML_DQK + 2 * ML_HEADS * ML_DV
    hn = rmsnorm(h, norm_g, BF16)
    proj = matmul(hn, w_in, j, n_main, BF16)
    gates = matmul(hn, w_in, j, LANE, F32, col_start=n_main, n_valid=2 * ML_HEADS)
    gates = gates[:, :2 * ML_HEADS].reshape(batch, tp, 2, ML_HEADS).transpose(2, 0, 3, 1)
    gates = gates.reshape(2, batch, ML_HEADS, 1, tp)
    y = mlstm_core(proj, gates[0], gates[1], b_i, b_f, h_gain, batch, tp)
    return matmul(y, w_out, j, d, F32, resid=h)


def fox_layer(h, batch, tp, j, norm_g, w_in, b_f, q_gain, k_gain, w_out):
    d = h.shape[1]
    n_main = 4 * d
    hn = rmsnorm(h, norm_g, BF16)
    head_gain = jnp.concatenate([jnp.tile(q_gain * (FOX_DH ** -0.5 * LOG2E), FOX_HEADS), jnp.tile(k_gain, FOX_HEADS),
                                 jnp.ones((2 * d,), F32)]).reshape(1, n_main)
    proj = matmul(hn, w_in, j, n_main, BF16, head_gain=head_gain, n_norm_cols=2 * d)
    fg = matmul(hn, w_in, j, LANE, F32, col_start=n_main, n_valid=FOX_HEADS)
    cb = fox_cumsum(fg, b_f, batch, tp)
    y = fox_attention(proj, cb, batch, tp)
    return matmul(y, w_out, j, d, F32, resid=h)


def kernel(x, meta_tokens, ml_norm, ml_w_in, ml_b_i, ml_b_f, ml_h_gain, ml_w_out, ffn_norm, ffn_w_gate, ffn_w_up, ffn_w_down, fox_norm, fox_w_in, fox_b_f, fox_q_gain, fox_k_gain, fox_w_out, moe_norm, moe_router, moe_w_gate, moe_w_up, moe_w_down, final_norm):
    batch, seq, d = x.shape
    tp = FRONT + seq
    depth = ml_norm.shape[0] + fox_norm.shape[0]
    front = jnp.concatenate([jnp.zeros((N_DUMMY, d), x.dtype), meta_tokens.astype(x.dtype)], axis=0)
    h = jnp.concatenate([jnp.broadcast_to(front[None], (batch, FRONT, d)), x], axis=1).reshape(batch * tp, d)
    bf = lambda w: w.astype(BF16)
    for i in range(depth):
        j = i // 2
        if i % 2 == 0:
            h = mlstm_layer(h, batch, tp, j, ml_norm[j], ml_w_in, ml_b_i[j], ml_b_f[j], ml_h_gain[j], ml_w_out)
            h = dense_ffn(h, ffn_norm[j], bf(ffn_w_gate), bf(ffn_w_up), bf(ffn_w_down), j)
        else:
            h = fox_layer(h, batch, tp, j, fox_norm[j], fox_w_in, fox_b_f[j], fox_q_gain[j], fox_k_gain[j], fox_w_out)
            h = moe_layer(h, moe_norm[j], moe_router[j], bf(moe_w_gate[j]), bf(moe_w_up[j]), bf(moe_w_down[j]))
    return final_rmsnorm(h.reshape(batch, tp, d), final_norm)
```
